```python
import jax, jax.numpy as jnp
from jax import lax
import numpy as np

D_MODEL = 1024
BATCH = 1
SEQ = 16384
DEPTH = 2
DEC_BATCH = 32
DEC_SEQ = 8
PAST_LEN = 16384
PAGE_SIZE = 128

N_HEADS = 16
HEAD_DIM = D_MODEL // N_HEADS
Q_BLOCK = 128
POOL_WINDOWS = (2, 4, 8, 16)
N_POOL_GROUPS = len(POOL_WINDOWS)
POOL_GROUP_DIM = D_MODEL // N_POOL_GROUPS
POOL_HIST = max(POOL_WINDOWS) - 1
D_FF = 7 * D_MODEL // 2
N_EXPERTS = 8
TOP_K = 2
RMS_EPS = 1e-6

kernel_name = "fox_pool_hybrid_decode_step"


def rmsnorm(x, gain):
    xf = x.astype(jnp.float32)
    y = xf * lax.rsqrt(jnp.mean(xf * xf, axis=-1, keepdims=True) + RMS_EPS)
    return (y * gain.astype(jnp.float32)).astype(x.dtype)


def swiglu(x, w_gate_up, w_down):
    gu = jnp.einsum('bsd,df->bsf', x, w_gate_up)
    g, u = jnp.split(gu, 2, axis=-1)
    return jnp.einsum('bsf,fd->bsd', jax.nn.silu(g) * u, w_down)


def moe_swiglu(x, w_router, w_gate_up, w_down):
    logits = jnp.einsum('bsd,de->bse', x, w_router).astype(jnp.float32)
    top_val, top_idx = lax.top_k(logits, TOP_K)
    top_w = jax.nn.softmax(top_val, axis=-1)
    gates = jnp.sum(jax.nn.one_hot(top_idx, N_EXPERTS, dtype=jnp.float32) * top_w[..., None], axis=-2)
    gates = gates.astype(x.dtype)
    out = jnp.zeros_like(x)
    for e in range(N_EXPERTS):
        out = out + gates[..., e:e + 1] * swiglu(x, w_gate_up[e], w_down[e])
    return out


def fox_project(xn, w_qkvf, b_f):
    B, S, _ = xn.shape
    h = jnp.einsum('bsd,df->bsf', xn, w_qkvf)
    q = h[..., :D_MODEL].reshape(B, S, N_HEADS, HEAD_DIM)
    k = h[..., D_MODEL:2 * D_MODEL].reshape(B, S, N_HEADS, HEAD_DIM)
    v = h[..., 2 * D_MODEL:3 * D_MODEL].reshape(B, S, N_HEADS, HEAD_DIM)
    logf = jax.nn.log_sigmoid((h[..., 3 * D_MODEL:] + b_f).astype(jnp.float32))
    return q, k, v, logf


def fox_attend(q, k, v, f_q, f_k, pos_q, pos_k):
    logits = jnp.einsum('bqhd,bkhd->bhqk', q, k, preferred_element_type=jnp.float32) * (HEAD_DIM ** -0.5)
    bias = jnp.transpose(f_q, (0, 2, 1))[..., :, None] - jnp.transpose(f_k, (0, 2, 1))[..., None, :]
    mask = pos_k[None, :] <= pos_q[:, None]
    logits = jnp.where(mask, logits + bias, -jnp.inf)
    p = jax.nn.softmax(logits, axis=-1)
    return jnp.einsum('bhqk,bkhd->bqhd', p.astype(v.dtype), v)


def fox_prompt(xn, w_qkvf, b_f, w_o):
    B, S, _ = xn.shape
    q, k, v, logf = fox_project(xn, w_qkvf, b_f)
    f_cum = jnp.cumsum(logf, axis=1)
    pos_k = jnp.arange(S)

    def one_block(i):
        start = i * Q_BLOCK
        qb = lax.dynamic_slice_in_dim(q, start, Q_BLOCK, axis=1)
        fq = lax.dynamic_slice_in_dim(f_cum, start, Q_BLOCK, axis=1)
        pos_q = start + jnp.arange(Q_BLOCK)
        return fox_attend(qb, k, v, fq, f_cum, pos_q, pos_k)

    out = lax.map(one_block, jnp.arange(S // Q_BLOCK))
    out = jnp.moveaxis(out, 0, 1).reshape(B, S, D_MODEL)
    return jnp.einsum('bsd,de->bse', out, w_o), k, v, logf


def fox_sample(xn, cache_k, cache_v, cache_logf, page_table, w_qkvf, b_f, w_o):
    B, T, _ = xn.shape
    past_len = page_table.shape[1] * cache_k.shape[1]
    q, k, v, logf = fox_project(xn, w_qkvf, b_f)
    k_past = cache_k[page_table].reshape(B, past_len, N_HEADS, HEAD_DIM)
    v_past = cache_v[page_table].reshape(B, past_len, N_HEADS, HEAD_DIM)
    lf_past = cache_logf[page_table].reshape(B, past_len, N_HEADS).astype(jnp.float32)
    k_all = jnp.concatenate([k_past, k.astype(k_past.dtype)], axis=1)
    v_all = jnp.concatenate([v_past, v.astype(v_past.dtype)], axis=1)
    f_cum = jnp.cumsum(jnp.concatenate([lf_past, logf], axis=1), axis=1)
    pos_q = past_len + jnp.arange(T)
    pos_k = jnp.arange(past_len + T)
    out = fox_attend(q, k_all, v_all, f_cum[:, past_len:], f_cum, pos_q, pos_k).reshape(B, T, D_MODEL)
    return jnp.einsum('bsd,de->bse', out, w_o), k, v, logf


def pool_mix(u_ext, pos, w_group, scale):
    B, L, D = u_ext.shape
    H = POOL_HIST
    N = L - H
    cs = jnp.concatenate([jnp.zeros((B, 1, D), jnp.float32), jnp.cumsum(u_ext.astype(jnp.float32), axis=1)], axis=1)
    u_cur = u_ext[:, H:].astype(jnp.float32)
    outs = []
    for g, w in enumerate(POOL_WINDOWS):
        lo, hi = g * POOL_GROUP_DIM, (g + 1) * POOL_GROUP_DIM
        win_sum = cs[:, H + 1:H + 1 + N, lo:hi] - cs[:, H + 1 - w:H + 1 - w + N, lo:hi]
        count = jnp.minimum(pos + 1, w).astype(jnp.float32)[None, :, None]
        pooled = (win_sum / count - u_cur[..., lo:hi]).astype(u_ext.dtype)
        outs.append(jnp.einsum('bnc,ce->bne', pooled, w_group[g]))
    return jnp.concatenate(outs, axis=-1) * scale


def setup_inputs(seed: int = 0) -> dict:
    key = jax.random.key(seed)
    ks = jax.random.split(key, 24)
    n_pages = PAST_LEN // PAGE_SIZE
    n_used = DEC_BATCH * n_pages
    n_phys = n_used + (n_used + 3) // 4
    f32 = jnp.float32

    def nrm(k, shape, s=1.0):
        return jax.random.normal(k, shape, f32) * s

    page_table = jax.random.permutation(ks[6], n_phys)[:n_used].reshape(DEC_BATCH, n_pages).astype(jnp.int32)
    return {
        "x_prompt": nrm(ks[0], (BATCH, SEQ, D_MODEL)),
        "x_sample": nrm(ks[1], (DEC_BATCH, DEC_SEQ, D_MODEL)),
        "cache_k": nrm(ks[2], (n_phys, PAGE_SIZE, N_HEADS, HEAD_DIM)),
        "cache_v": nrm(ks[3], (n_phys, PAGE_SIZE, N_HEADS, HEAD_DIM)),
        "cache_logf": jax.nn.log_sigmoid(4.0 + nrm(ks[4], (n_phys, PAGE_SIZE, N_HEADS))),
        "state_pool": nrm(ks[5], (DEC_BATCH, POOL_HIST, D_MODEL)),
        "page_table": page_table,
        "l0_norm_attn": 1.0 + nrm(ks[7], (D_MODEL,), 0.01),
        "l0_w_qkvf": nrm(ks[8], (D_MODEL, 3 * D_MODEL + N_HEADS), D_MODEL ** -0.5),
        "l0_b_f": jnp.linspace(1.0, 6.0, N_HEADS, dtype=f32) + nrm(ks[9], (N_HEADS,), 0.1),
        "l0_w_o": nrm(ks[10], (D_MODEL, D_MODEL), D_MODEL ** -0.5),
        "l0_norm_ffn": 1.0 + nrm(ks[11], (D_MODEL,), 0.01),
        "l0_w_gate_up": nrm(ks[12], (D_MODEL, 2 * D_FF), D_MODEL ** -0.5),
        "l0_w_down": nrm(ks[13], (D_FF, D_MODEL), D_FF ** -0.5),
        "l1_norm_pool": 1.0 + nrm(ks[14], (D_MODEL,), 0.01),
        "l1_w_group": nrm(ks[15], (N_POOL_GROUPS, POOL_GROUP_DIM, POOL_GROUP_DIM), POOL_GROUP_DIM ** -0.5),
        "l1_pool_scale": 0.5 + nrm(ks[16], (D_MODEL,), 0.05),
        "l1_norm_ffn": 1.0 + nrm(ks[17], (D_MODEL,), 0.01),
        "l1_w_router": nrm(ks[18], (D_MODEL, N_EXPERTS), D_MODEL ** -0.5),
        "l1_w_gate_up": nrm(ks[19], (N_EXPERTS, D_MODEL, 2 * D_FF), D_MODEL ** -0.5),
        "l1_w_down": nrm(ks[20], (N_EXPERTS, D_FF, D_MODEL), D_FF ** -0.5),
        "final_norm": 1.0 + nrm(ks[21], (D_MODEL,), 0.01),
    }


def reference(x_prompt, x_sample, cache_k, cache_v, cache_logf, state_pool, page_table,
              l0_norm_attn, l0_w_qkvf, l0_b_f, l0_w_o, l0_norm_ffn, l0_w_gate_up, l0_w_down,
              l1_norm_pool, l1_w_group, l1_pool_scale, l1_norm_ffn, l1_w_router, l1_w_gate_up, l1_w_down,
              final_norm):
    past_len = page_table.shape[1] * cache_k.shape[1]
    hp, hs = x_prompt, x_sample
    for layer in range(DEPTH):
        if layer % 2 == 0:
            a_p, k_p, v_p, lf_p = fox_prompt(rmsnorm(hp, l0_norm_attn), l0_w_qkvf, l0_b_f, l0_w_o)
            a_s, k_s, v_s, lf_s = fox_sample(rmsnorm(hs, l0_norm_attn), cache_k, cache_v, cache_logf,
                                             page_table, l0_w_qkvf, l0_b_f, l0_w_o)
            hp = hp + a_p
            hs = hs + a_s
            hp = hp + swiglu(rmsnorm(hp, l0_norm_ffn), l0_w_gate_up, l0_w_down)
            hs = hs + swiglu(rmsnorm(hs, l0_norm_ffn), l0_w_gate_up, l0_w_down)
        else:
            up = rmsnorm(hp, l1_norm_pool)
            us = rmsnorm(hs, l1_norm_pool)
            up_ext = jnp.concatenate([jnp.zeros((up.shape[0], POOL_HIST, D_MODEL), up.dtype), up], axis=1)
            us_ext = jnp.concatenate([state_pool.astype(us.dtype), us], axis=1)
            hp = hp + pool_mix(up_ext, jnp.arange(up.shape[1]), l1_w_group, l1_pool_scale)
            hs = hs + pool_mix(us_ext, past_len + jnp.arange(us.shape[1]), l1_w_group, l1_pool_scale)
            pool_p = up_ext[:, -POOL_HIST:]
            pool_s = us_ext[:, -POOL_HIST:]
            hp = hp + moe_swiglu(rmsnorm(hp, l1_norm_ffn), l1_w_router, l1_w_gate_up, l1_w_down)
            hs = hs + moe_swiglu(rmsnorm(hs, l1_norm_ffn), l1_w_router, l1_w_gate_up, l1_w_down)
    y_prompt = rmsnorm(hp, final_norm)
    y_sample = rmsnorm(hs, final_norm)
    return (y_prompt, y_sample, k_p, v_p, lf_p, pool_p, k_s, v_s, lf_s, pool_s)
```

```python
import functools

import jax
import jax.numpy as jnp
from jax import lax
from jax.experimental import pallas as pl
from jax.experimental.pallas import tpu as pltpu

N_HEADS = 16
HEAD_DIM = 64
RMS_EPS = 1e-6
POOL_WINDOWS = (2, 4, 8, 16)
POOL_HIST = max(POOL_WINDOWS) - 1
POOL_PAD = POOL_HIST + 1
N_EXPERTS = 8
LANES = 128
NEG_BIG = -1e30
VMEM_LIMIT_BYTES = 56 * 1024 * 1024

F32 = jnp.float32
BF16 = jnp.bfloat16
_NT = (((1,), (1,)), ((), ()))


def _params(*sem):
    return pltpu.CompilerParams(dimension_semantics=sem, vmem_limit_bytes=VMEM_LIMIT_BYTES)


def _rmsnorm(x, gain):
    ms = jnp.mean(x * x, axis=-1, keepdims=True)
    return x * lax.rsqrt(ms + RMS_EPS) * gain


def _split3(x):
    hi = x.astype(BF16)
    r1 = x - hi.astype(F32)
    mid = r1.astype(BF16)
    lo = (r1 - mid.astype(F32)).astype(BF16)
    return hi, mid, lo


def _dot(a, b):
    return jnp.dot(a, b, preferred_element_type=F32)


def _dot_nt(a, b):
    return lax.dot_general(a, b, _NT, preferred_element_type=F32)


def _dot_exact_rhs(a_bf16, x_f32):
    hi, mid, lo = _split3(x_f32)
    return (_dot(a_bf16, hi) + _dot(a_bf16, mid)) + _dot(a_bf16, lo)


def _dot_exact_lhs(x_f32, a_bf16):
    hi, mid, lo = _split3(x_f32)
    return (_dot(hi, a_bf16) + _dot(mid, a_bf16)) + _dot(lo, a_bf16)


def _log_sigmoid(z):
    return jnp.minimum(z, 0.0) - jnp.log1p(jnp.exp(-jnp.abs(z)))


def _qkvf_kernel(x_ref, g_ref, wq_ref, wk_ref, wv_ref, wf_ref, wft_ref, bf_ref, bft_ref,
                 qh_ref, kh_ref, vh_ref, k_ref, v_ref, lf_ref, lft_ref):
    xn = _rmsnorm(x_ref[...], g_ref[...]).astype(BF16)
    q = _dot(xn, wq_ref[...]) * (HEAD_DIM ** -0.5)
    k = _dot(xn, wk_ref[...])
    v = _dot(xn, wv_ref[...])
    k_ref[...] = k
    v_ref[...] = v
    for h in range(N_HEADS):
        sl = slice(h * HEAD_DIM, (h + 1) * HEAD_DIM)
        qh_ref[h] = q[:, sl].astype(BF16)
        kh_ref[h] = k[:, sl].astype(BF16)
        vh_ref[h] = v[:, sl].astype(BF16)
    lf_ref[...] = _log_sigmoid(_dot(xn, wf_ref[...]) + bf_ref[...])
    lft_ref[...] = _log_sigmoid(_dot_nt(wft_ref[...], xn) + bft_ref[...])


def _qkvf(x, gain, wq, wk, wv, wf, wft, bf, bft, tm):
    n, d = x.shape
    full = lambda shape: pl.BlockSpec(shape, lambda i: (0,) * len(shape))
    hm = jax.ShapeDtypeStruct((N_HEADS, n, HEAD_DIM), BF16)
    hm_spec = pl.BlockSpec((N_HEADS, tm, HEAD_DIM), lambda i: (0, i, 0))
    row_spec = pl.BlockSpec((tm, d), lambda i: (i, 0))
    return pl.pallas_call(
        _qkvf_kernel,
        grid=(n // tm,),
        in_specs=[row_spec, full((1, d)), full((d, d)), full((d, d)), full((d, d)),
                  full((d, LANES)), full((N_HEADS, d)), full((1, LANES)), full((N_HEADS, 1))],
        out_specs=[hm_spec, hm_spec, hm_spec, row_spec, row_spec,
                   pl.BlockSpec((tm, LANES), lambda i: (i, 0)),
                   pl.BlockSpec((N_HEADS, tm), lambda i: (0, i))],
        out_shape=[hm, hm, hm,
                   jax.ShapeDtypeStruct((n, d), F32), jax.ShapeDtypeStruct((n, d), F32),
                   jax.ShapeDtypeStruct((n, LANES), F32), jax.ShapeDtypeStruct((N_HEADS, n), F32)],
        compiler_params=_params("parallel"),
        name="qkvf_proj",
    )(x, gain, wq, wk, wv, wf, wft, bf, bft)


def _cumsum_kernel(lf_ref, lft_ref, fc_ref, fct_ref, c_ref, ct_ref):
    @pl.when(pl.program_id(0) == 0)
    def _():
        c_ref[...] = jnp.zeros_like(c_ref)
        ct_ref[...] = jnp.zeros_like(ct_ref)

    t = lf_ref.shape[0]
    row = lax.broadcasted_iota(jnp.int32, (t, t), 0)
    col = lax.broadcasted_iota(jnp.int32, (t, t), 1)
    lower = (col <= row).astype(BF16)
    upper = (row <= col).astype(BF16)
    fc = _dot_exact_rhs(lower, lf_ref[...]) + c_ref[...]
    fc_ref[...] = fc
    c_ref[...] = fc[t - 1:t, :]
    fct = _dot_exact_lhs(lft_ref[...], upper) + ct_ref[...]
    fct_ref[...] = fct
    ct_ref[...] = fct[:, t - 1:t]


def _cumsum(lf, lft, t):
    s = lf.shape[0]
    return pl.pallas_call(
        _cumsum_kernel,
        grid=(s // t,),
        in_specs=[pl.BlockSpec((t, LANES), lambda i: (i, 0)),
                  pl.BlockSpec((N_HEADS, t), lambda i: (0, i))],
        out_specs=[pl.BlockSpec((t, LANES), lambda i: (i, 0)),
                   pl.BlockSpec((N_HEADS, t), lambda i: (0, i))],
        out_shape=[jax.ShapeDtypeStruct((s, LANES), F32), jax.ShapeDtypeStruct((N_HEADS, s), F32)],
        scratch_shapes=[pltpu.VMEM((1, LANES), F32), pltpu.VMEM((N_HEADS, 1), F32)],
        compiler_params=_params("arbitrary"),
        name="logf_cumsum",
    )(lf, lft)


def _prompt_attn_kernel(q_ref, k_ref, v_ref, fq_ref, fk_ref, o_ref, m_sc, l_sc, acc_sc, *, blk):
    h = pl.program_id(0)
    qi = pl.program_id(1)
    q = q_ref[0]
    lane = lax.broadcasted_iota(jnp.int32, (blk, LANES), 1)
    fq = jnp.sum(jnp.where(lane == h, fq_ref[...], 0.0), axis=1, keepdims=True)
    m_sc[...] = jnp.full_like(m_sc, NEG_BIG)
    l_sc[...] = jnp.zeros_like(l_sc)
    acc_sc[...] = jnp.zeros_like(acc_sc)

    def step(ki, causal):
        ks = pl.multiple_of(ki * blk, blk)
        k = k_ref[0, pl.ds(ks, blk), :]
        v = v_ref[0, pl.ds(ks, blk), :]
        fk = fk_ref[0, :, pl.ds(ks, blk)]
        s = _dot_nt(q, k) + (fq - fk)
        if causal:
            row = lax.broadcasted_iota(jnp.int32, (blk, blk), 0)
            col = lax.broadcasted_iota(jnp.int32, (blk, blk), 1)
            s = jnp.where(col <= row, s, NEG_BIG)
        m_prev = m_sc[...]
        m_new = jnp.maximum(m_prev, jnp.max(s, axis=1, keepdims=True))
        alpha = jnp.exp(m_prev - m_new)
        p = jnp.exp(s - m_new)
        l_sc[...] = alpha * l_sc[...] + jnp.sum(p, axis=1, keepdims=True)
        acc_sc[...] = alpha * acc_sc[...] + _dot(p.astype(BF16), v)
        m_sc[...] = m_new

    def body(ki, carry):
        step(ki, False)
        return carry

    lax.fori_loop(0, qi, body, 0)
    step(qi, True)
    o_ref[0] = (acc_sc[...] / l_sc[...]).astype(o_ref.dtype)


def _prompt_attn(qh, kh, vh, fcum, fcum_t3, blk):
    _, s, hd = qh.shape
    return pl.pallas_call(
        functools.partial(_prompt_attn_kernel, blk=blk),
        grid=(N_HEADS, s // blk),
        in_specs=[pl.BlockSpec((1, blk, hd), lambda h, i: (h, i, 0)),
                  pl.BlockSpec((1, s, hd), lambda h, i: (h, 0, 0)),
                  pl.BlockSpec((1, s, hd), lambda h, i: (h, 0, 0)),
                  pl.BlockSpec((blk, LANES), lambda h, i: (i, 0)),
                  pl.BlockSpec((1, 1, s), lambda h, i: (h, 0, 0))],
        out_specs=pl.BlockSpec((1, blk, hd), lambda h, i: (h, i, 0)),
        out_shape=jax.ShapeDtypeStruct((N_HEADS, s, hd), BF16),
        scratch_shapes=[pltpu.VMEM((blk, 1), F32), pltpu.VMEM((blk, 1), F32),
                        pltpu.VMEM((blk, hd), F32)],
        compiler_params=_params("parallel", "parallel"),
        name="fox_prompt_attn",
    )(qh, kh, vh, fcum, fcum_t3)


def _sample_attn_kernel(pt_ref, *refs, n_pp, page, t_new):
    del pt_ref
    k_refs = refs[:n_pp]
    v_refs = refs[n_pp:2 * n_pp]
    lf_refs = refs[2 * n_pp:3 * n_pp]
    q_ref, kn_ref, vn_ref, lfn_ref, o_ref, m_sc, l_sc, acc_sc, suf_sc, cn_sc = refs[3 * n_pp:]
    j = pl.program_id(1)
    rows = N_HEADS * t_new
    q_all = q_ref[...].reshape(rows, HEAD_DIM)

    def pair_rows(x, h):
        return x[(h // 2) * 2 * t_new:(h // 2 + 1) * 2 * t_new, :]

    def own_rows(x_pair, h):
        return x_pair[(h % 2) * t_new:(h % 2 + 1) * t_new, :]

    def scores(h, keys_bf16):
        return own_rows(_dot_nt(pair_rows(q_all, h).astype(BF16), keys_bf16), h)

    def online_update(s, values_fn):
        m_prev = m_sc[...]
        m_new = jnp.maximum(m_prev, jnp.max(s, axis=1, keepdims=True))
        alpha = jnp.exp(m_prev - m_new)
        p = jnp.exp(s - m_new)
        l_sc[...] = alpha * l_sc[...] + jnp.sum(p, axis=1, keepdims=True)
        pv = jnp.concatenate(
            [own_rows(_dot(pair_rows(p, h).astype(BF16), values_fn(h)), h) for h in range(N_HEADS)], axis=0)
        acc_sc[...] = alpha * acc_sc[...] + pv
        m_sc[...] = m_new

    @pl.when(j == 0)
    def _():
        m_sc[...] = jnp.full_like(m_sc, NEG_BIG)
        l_sc[...] = jnp.zeros_like(l_sc)
        acc_sc[...] = jnp.zeros_like(acc_sc)
        suf_sc[...] = jnp.zeros_like(suf_sc)
        r8 = lax.broadcasted_iota(jnp.int32, (t_new, t_new), 0)
        c8 = lax.broadcasted_iota(jnp.int32, (t_new, t_new), 1)
        cum = _dot_exact_rhs((c8 <= r8).astype(BF16), lfn_ref[...])
        pad = jnp.zeros((LANES - t_new, LANES), F32)
        cum_pad = jnp.concatenate([cum, pad], axis=0)
        eye = (lax.broadcasted_iota(jnp.int32, (N_HEADS, LANES), 0)
               == lax.broadcasted_iota(jnp.int32, (N_HEADS, LANES), 1)).astype(BF16)
        hi, mid, lo = _split3(cum_pad)
        cum_t = (_dot_nt(eye, hi) + _dot_nt(eye, mid)) + _dot_nt(eye, lo)
        zpad = jnp.zeros((LANES - t_new, HEAD_DIM), F32)
        s_parts = []
        for h in range(N_HEADS):
            cn_h = cum[:, h:h + 1]
            cn_sc[h * t_new:(h + 1) * t_new, :] = cn_h
            kn = jnp.concatenate([kn_ref[h], zpad], axis=0).astype(BF16)
            s_parts.append(scores(h, kn) + (cn_h - cum_t[h:h + 1, :]))
        s = jnp.concatenate(s_parts, axis=0)
        qt = lax.broadcasted_iota(jnp.int32, (rows, LANES), 0) % t_new
        kt = lax.broadcasted_iota(jnp.int32, (rows, LANES), 1)
        s = jnp.where(kt <= qt, s, NEG_BIG)
        online_update(s, lambda h: jnp.concatenate([vn_ref[h], zpad], axis=0).astype(BF16))

    rp = lax.broadcasted_iota(jnp.int32, (page, page), 0)
    cp = lax.broadcasted_iota(jnp.int32, (page, page), 1)
    newer = (rp > cp).astype(BF16)
    carry = suf_sc[...]
    suf_pages = [None] * n_pp
    for i in reversed(range(n_pp)):
        lf = lf_refs[i][...]
        suf_pages[i] = _dot_exact_lhs(lf, newer) + carry
        carry = carry + jnp.sum(lf, axis=1, keepdims=True)
    suf_sc[...] = carry
    suf = jnp.concatenate(suf_pages, axis=1)

    def head_rows(page_refs, h):
        parts = [r[pl.ds(h, page, stride=N_HEADS), :] for r in page_refs]
        return jnp.concatenate(parts, axis=0).astype(BF16)

    s_parts = []
    for h in range(N_HEADS):
        s_parts.append(scores(h, head_rows(k_refs, h))
                       + (suf[h:h + 1, :] + cn_sc[h * t_new:(h + 1) * t_new, :]))
    s = jnp.concatenate(s_parts, axis=0)
    online_update(s, lambda h: head_rows(v_refs, h))

    @pl.when(j == pl.num_programs(1) - 1)
    def _():
        out = acc_sc[...] / l_sc[...]
        for h in range(N_HEADS):
            o_ref[h] = out[h * t_new:(h + 1) * t_new, :]


def _sample_attn(page_table, cache_k, cache_v, cache_lf_t, q_s, k_s, v_s, lf_s, n_pp):
    n_seq, n_pages = page_table.shape
    page = cache_k.shape[1] // N_HEADS
    t_new = q_s.shape[1] // n_seq
    n_grp = n_pages // n_pp

    def page_idx(i):
        return lambda b, j, pt: (pt[b, (n_grp - 1 - j) * n_pp + i], 0, 0)

    kv_specs = [pl.BlockSpec((None, page * N_HEADS, HEAD_DIM), page_idx(i)) for i in range(n_pp)]
    lf_specs = [pl.BlockSpec((None, N_HEADS, page), page_idx(i)) for i in range(n_pp)]
    tok_spec = pl.BlockSpec((N_HEADS, t_new, HEAD_DIM), lambda b, j, pt: (0, b, 0))
    grid_spec = pltpu.PrefetchScalarGridSpec(
        num_scalar_prefetch=1,
        grid=(n_seq, n_grp),
        in_specs=kv_specs + kv_specs + lf_specs + [
            tok_spec, tok_spec, tok_spec, pl.BlockSpec((t_new, LANES), lambda b, j, pt: (b, 0))],
        out_specs=tok_spec,
        scratch_shapes=[pltpu.VMEM((N_HEADS * t_new, 1), F32), pltpu.VMEM((N_HEADS * t_new, 1), F32),
                        pltpu.VMEM((N_HEADS * t_new, HEAD_DIM), F32), pltpu.VMEM((N_HEADS, 1), F32),
                        pltpu.VMEM((N_HEADS * t_new, 1), F32)],
    )
    return pl.pallas_call(
        functools.partial(_sample_attn_kernel, n_pp=n_pp, page=page, t_new=t_new),
        grid_spec=grid_spec,
        out_shape=jax.ShapeDtypeStruct((N_HEADS, n_seq * t_new, HEAD_DIM), F32),
        compiler_params=_params("parallel", "arbitrary"),
        name="fox_sample_attn",
    )(page_table, *([cache_k] * n_pp), *([cache_v] * n_pp), *([cache_lf_t] * n_pp), q_s, k_s, v_s, lf_s)


def _oproj_kernel(a_ref, x_ref, wo_ref, o_ref):
    acc = x_ref[...]
    for h in range(N_HEADS):
        acc = acc + _dot(a_ref[h].astype(BF16), wo_ref[h])
    o_ref[...] = acc


def _oproj(attn_hm, x, wo_hm, tm):
    n, d = x.shape
    row_spec = pl.BlockSpec((tm, d), lambda i: (i, 0))
    return pl.pallas_call(
        _oproj_kernel,
        grid=(n // tm,),
        in_specs=[pl.BlockSpec((N_HEADS, tm, HEAD_DIM), lambda i: (0, i, 0)), row_spec,
                  pl.BlockSpec((N_HEADS, HEAD_DIM, d), lambda i: (0, 0, 0))],
        out_specs=row_spec,
        out_shape=jax.ShapeDtypeStruct((n, d), F32),
        compiler_params=_params("parallel"),
        name="attn_out_proj",
    )(attn_hm, x, wo_hm)


def _ffn_kernel(x_ref, g_ref, wg_ref, wu_ref, wd_ref, o_ref, xn_sc, acc_sc):
    j = pl.program_id(1)

    @pl.when(j == 0)
    def _():
        xn_sc[...] = _rmsnorm(x_ref[...], g_ref[...]).astype(BF16)
        acc_sc[...] = jnp.zeros_like(acc_sc)

    xn = xn_sc[...]
    gate = _dot(xn, wg_ref[...])
    up = _dot(xn, wu_ref[...])
    act = (gate * jax.nn.sigmoid(gate) * up).astype(BF16)
    acc_sc[...] += _dot(act, wd_ref[...])

    @pl.when(j == pl.num_programs(1) - 1)
    def _():
        o_ref[...] = x_ref[...] + acc_sc[...]


def _ffn(x, gain, w_gate_up, w_down, tm, tf):
    n, d = x.shape
    d_ff = w_down.shape[0]
    n_f = d_ff // tf
    row_spec = pl.BlockSpec((tm, d), lambda i, j: (i, 0))
    return pl.pallas_call(
        _ffn_kernel,
        grid=(n // tm, n_f),
        in_specs=[row_spec, pl.BlockSpec((1, d), lambda i, j: (0, 0)),
                  pl.BlockSpec((d, tf), lambda i, j: (0, j)),
                  pl.BlockSpec((d, tf), lambda i, j: (0, j + n_f)),
                  pl.BlockSpec((tf, d), lambda i, j: (j, 0))],
        out_specs=row_spec,
        out_shape=jax.ShapeDtypeStruct((n, d), F32),
        scratch_shapes=[pltpu.VMEM((tm, d), BF16), pltpu.VMEM((tm, d), F32)],
        compiler_params=_params("parallel", "arbitrary"),
        name="dense_swiglu",
    )(x, gain, w_gate_up, w_gate_up, w_down)


def _pool_mix(ext_ref, x, u, pos0, wgrp_ref, scale_ref, o_ref, n_rows):
    gdim = u.shape[1] // len(POOL_WINDOWS)
    pos = pos0 + lax.broadcasted_iota(jnp.int32, (n_rows, 1), 0)
    for g, w in enumerate(POOL_WINDOWS):
        cols = pl.ds(g * gdim, gdim)
        win = ext_ref[pl.ds(POOL_PAD, n_rows), cols]
        for i in range(1, w):
            win = win + ext_ref[pl.ds(POOL_PAD - i, n_rows), cols]
        count = jnp.minimum(pos + 1, w).astype(F32)
        pooled = (win / count - u[:, g * gdim:(g + 1) * gdim]).astype(BF16)
        mixed = _dot(pooled, wgrp_ref[g]) * scale_ref[:, cols]
        o_ref[:, cols] = x[:, g * gdim:(g + 1) * gdim] + mixed


def _pool_prompt_kernel(x_ref, g_ref, wgrp_ref, scale_ref, o_ref, tail_ref, ext_sc, *, tm):
    i = pl.program_id(0)

    @pl.when(i == 0)
    def _():
        ext_sc[pl.ds(0, POOL_PAD), :] = jnp.zeros((POOL_PAD, ext_sc.shape[1]), F32)

    @pl.when(i > 0)
    def _():
        ext_sc[pl.ds(0, POOL_PAD), :] = ext_sc[pl.ds(tm, POOL_PAD), :]

    x = x_ref[...]
    u = _rmsnorm(x, g_ref[...])
    ext_sc[pl.ds(POOL_PAD, tm), :] = u
    tail_ref[...] = u[tm - POOL_PAD:, :]
    _pool_mix(ext_sc, x, u, i * tm, wgrp_ref, scale_ref, o_ref, tm)


def _pool_prompt(x, gain, w_group, scale, tm):
    n, d = x.shape
    gdim = d // len(POOL_WINDOWS)
    row_spec = pl.BlockSpec((tm, d), lambda i: (i, 0))
    return pl.pallas_call(
        functools.partial(_pool_prompt_kernel, tm=tm),
        grid=(n // tm,),
        in_specs=[row_spec, pl.BlockSpec((1, d), lambda i: (0, 0)),
                  pl.BlockSpec((len(POOL_WINDOWS), gdim, gdim), lambda i: (0, 0, 0)),
                  pl.BlockSpec((1, d), lambda i: (0, 0))],
        out_specs=[row_spec, pl.BlockSpec((POOL_PAD, d), lambda i: (0, 0))],
        out_shape=[jax.ShapeDtypeStruct((n, d), F32), jax.ShapeDtypeStruct((POOL_PAD, d), F32)],
        scratch_shapes=[pltpu.VMEM((POOL_PAD + tm, d), F32)],
        compiler_params=_params("arbitrary"),
        name="pool_mix_prompt",
    )(x, gain, w_group, scale)


def _pool_sample_kernel(x_ref, st_ref, g_ref, wgrp_ref, scale_ref, o_ref, hist_ref, ext_sc, *, t_new, past_len):
    x = x_ref[...]
    u = _rmsnorm(x, g_ref[...])
    ext_sc[pl.ds(0, POOL_PAD), :] = st_ref[...]
    ext_sc[pl.ds(POOL_PAD, t_new), :] = u
    hist_ref[...] = ext_sc[pl.ds(t_new, POOL_PAD), :]
    _pool_mix(ext_sc, x, u, past_len, wgrp_ref, scale_ref, o_ref, t_new)


def _pool_sample(x, state_pad, gain, w_group, scale, t_new, past_len):
    n, d = x.shape
    gdim = d // len(POOL_WINDOWS)
    row_spec = pl.BlockSpec((t_new, d), lambda b: (b, 0))
    st_spec = pl.BlockSpec((None, POOL_PAD, d), lambda b: (b, 0, 0))
    return pl.pallas_call(
        functools.partial(_pool_sample_kernel, t_new=t_new, past_len=past_len),
        grid=(n // t_new,),
        in_specs=[row_spec, st_spec, pl.BlockSpec((1, d), lambda b: (0, 0)),
                  pl.BlockSpec((len(POOL_WINDOWS), gdim, gdim), lambda b: (0, 0, 0)),
                  pl.BlockSpec((1, d), lambda b: (0, 0))],
        out_specs=[row_spec, st_spec],
        out_shape=[jax.ShapeDtypeStruct((n, d), F32), jax.ShapeDtypeStruct(state_pad.shape, F32)],
        scratch_shapes=[pltpu.VMEM((POOL_PAD + t_new, d), F32)],
        compiler_params=_params("parallel"),
        name="pool_mix_sample",
    )(x, state_pad, gain, w_group, scale)


def _moe_kernel(x_ref, g_ref, wr_ref, wg_ref, wu_ref, wd_ref, gf_ref, o_ref, xn_sc, gates_sc, acc_sc):
    e = pl.program_id(1)
    j = pl.program_id(2)
    tm = x_ref.shape[0]
    lane = lax.broadcasted_iota(jnp.int32, (tm, LANES), 1)

    @pl.when((e == 0) & (j == 0))
    def _():
        xn = _rmsnorm(x_ref[...], g_ref[...])
        xh, xm, xl = _split3(xn)
        xn_sc[...] = xh
        wh, wm, wl = wr_ref[0], wr_ref[1], wr_ref[2]
        logits = ((_dot(xh, wl) + _dot(xm, wm) + _dot(xl, wh))
                  + (_dot(xh, wm) + _dot(xm, wh))) + _dot(xh, wh)
        logits = jnp.where(lane < N_EXPERTS, logits, -jnp.inf)
        top1 = jnp.max(logits, axis=1, keepdims=True)
        idx1 = jnp.min(jnp.where(logits == top1, lane, LANES), axis=1, keepdims=True)
        rest = jnp.where(lane == idx1, -jnp.inf, logits)
        top2 = jnp.max(rest, axis=1, keepdims=True)
        idx2 = jnp.min(jnp.where(rest == top2, lane, LANES), axis=1, keepdims=True)
        e2 = jnp.exp(top2 - top1)
        denom = 1.0 + e2
        gates_sc[...] = jnp.where(lane == idx1, 1.0 / denom, 0.0) + jnp.where(lane == idx2, e2 / denom, 0.0)
        acc_sc[...] = jnp.zeros_like(acc_sc)

    gate_e = jnp.sum(jnp.where(lane == e, gates_sc[...], 0.0), axis=1, keepdims=True)
    xn = xn_sc[...]
    gate = _dot(xn, wg_ref[...])
    up = _dot(xn, wu_ref[...])
    act = (gate * jax.nn.sigmoid(gate) * up * gate_e).astype(BF16)
    acc_sc[...] += _dot(act, wd_ref[...])

    @pl.when((e == pl.num_programs(1) - 1) & (j == pl.num_programs(2) - 1))
    def _():
        o_ref[...] = _rmsnorm(x_ref[...] + acc_sc[...], gf_ref[...])


def _moe(x, gain, w_router3, w_gate_up, w_down, gain_final, tm, tf):
    n, d = x.shape
    n_e, d_ff, _ = w_down.shape
    n_f = d_ff // tf
    row_spec = pl.BlockSpec((tm, d), lambda i, e, j: (i, 0))
    vec_spec = pl.BlockSpec((1, d), lambda i, e, j: (0, 0))
    return pl.pallas_call(
        _moe_kernel,
        grid=(n // tm, n_e, n_f),
        in_specs=[row_spec, vec_spec,
                  pl.BlockSpec((3, d, LANES), lambda i, e, j: (0, 0, 0)),
                  pl.BlockSpec((None, d, tf), lambda i, e, j: (e, 0, j)),
                  pl.BlockSpec((None, d, tf), lambda i, e, j: (e, 0, j + n_f)),
                  pl.BlockSpec((None, tf, d), lambda i, e, j: (e, j, 0)),
                  vec_spec],
        out_specs=row_spec,
        out_shape=jax.ShapeDtypeStruct((n, d), F32),
        scratch_shapes=[pltpu.VMEM((tm, d), BF16), pltpu.VMEM((tm, LANES), F32), pltpu.VMEM((tm, d), F32)],
        compiler_params=_params("parallel", "arbitrary", "arbitrary"),
        name="moe_swiglu_final_norm",
    )(x, gain, w_router3, w_gate_up, w_gate_up, w_down, gain_final)


def _row_tile(n, target):
    t = min(n, target)
    while n % t:
        t //= 2
    return t


def kernel(x_prompt, x_sample, cache_k, cache_v, cache_logf, state_pool, page_table, l0_norm_attn, l0_w_qkvf, l0_b_f, l0_w_o, l0_norm_ffn, l0_w_gate_up, l0_w_down, l1_norm_pool, l1_w_group, l1_pool_scale, l1_norm_ffn, l1_w_router, l1_w_gate_up, l1_w_down, final_norm):
    b_p, s_p, d = x_prompt.shape
    b_s, t_new, _ = x_sample.shape
    assert b_p == 1 and d == N_HEADS * HEAD_DIM
    page = cache_k.shape[1]
    past_len = page_table.shape[1] * page
    d_ff = l0_w_down.shape[0]

    row = lambda v: v.reshape(1, -1).astype(F32)
    wq = l0_w_qkvf[:, :d].astype(BF16)
    wk = l0_w_qkvf[:, d:2 * d].astype(BF16)
    wv = l0_w_qkvf[:, 2 * d:3 * d].astype(BF16)
    wf_cols = l0_w_qkvf[:, 3 * d:]
    wf = jnp.pad(wf_cols, ((0, 0), (0, LANES - N_HEADS))).astype(BF16)
    wft = wf_cols.T.astype(BF16)
    bf = jnp.pad(l0_b_f, (0, LANES - N_HEADS)).reshape(1, LANES)
    bft = l0_b_f.reshape(N_HEADS, 1)
    wo_hm = l0_w_o.astype(BF16).reshape(N_HEADS, HEAD_DIM, d)
    w0_gu = l0_w_gate_up.astype(BF16)
    w0_d = l0_w_down.astype(BF16)
    wgrp = l1_w_group.astype(BF16)
    wr_pad = jnp.pad(l1_w_router, ((0, 0), (0, LANES - N_EXPERTS)))
    wr_hi = wr_pad.astype(BF16)
    wr_r1 = wr_pad - wr_hi.astype(F32)
    wr_mid = wr_r1.astype(BF16)
    wr_lo = (wr_r1 - wr_mid.astype(F32)).astype(BF16)
    wr3 = jnp.stack([wr_hi, wr_mid, wr_lo])
    w1_gu = l1_w_gate_up.astype(BF16)
    w1_d = l1_w_down.astype(BF16)
    tf = 512 if d_ff % 512 == 0 else d_ff

    xp = x_prompt.reshape(s_p, d)
    xs = x_sample.reshape(b_s * t_new, d)
    n_s = b_s * t_new

    proj = lambda x, tm: _qkvf(x, row(l0_norm_attn), wq, wk, wv, wf, wft, bf, bft, tm)
    qh_p, kh_p, vh_p, k_p, v_p, lf_p, lft_p = proj(xp, _row_tile(s_p, 512))
    qh_s, kh_s, vh_s, k_s, v_s, lf_s, _ = proj(xs, _row_tile(n_s, 512))

    blk = _row_tile(s_p, 512)
    fcum, fcum_t = _cumsum(lf_p, lft_p, blk)
    attn_p = _prompt_attn(qh_p, kh_p, vh_p, fcum, fcum_t.reshape(N_HEADS, 1, s_p), blk)

    n_pp = 8 if page_table.shape[1] % 8 == 0 else 1
    rows_view = lambda c: c.reshape(c.shape[0], page * N_HEADS, HEAD_DIM)
    attn_s = _sample_attn(page_table, rows_view(cache_k), rows_view(cache_v), jnp.swapaxes(cache_logf, 1, 2),
                          qh_s.astype(F32), kh_s.astype(F32), vh_s.astype(F32), lf_s, n_pp)

    hp = _oproj(attn_p, xp, wo_hm, _row_tile(s_p, 512))
    hs = _oproj(attn_s, xs, wo_hm, _row_tile(n_s, 512))

    hp = _ffn(hp, row(l0_norm_ffn), w0_gu, w0_d, _row_tile(s_p, 1024), tf)
    hs = _ffn(hs, row(l0_norm_ffn), w0_gu, w0_d, _row_tile(n_s, 1024), tf)

    hp, tail_p = _pool_prompt(hp, row(l1_norm_pool), wgrp, row(l1_pool_scale), _row_tile(s_p, 512))
    state_pad = jnp.pad(state_pool, ((0, 0), (POOL_PAD - POOL_HIST, 0), (0, 0)))
    hs, hist_s = _pool_sample(hs, state_pad, row(l1_norm_pool), wgrp, row(l1_pool_scale), t_new, past_len)

    yp = _moe(hp, row(l1_norm_ffn), wr3, w1_gu, w1_d, row(final_norm), _row_tile(s_p, 1024), tf)
    ys = _moe(hs, row(l1_norm_ffn), wr3, w1_gu, w1_d, row(final_norm), _row_tile(n_s, 1024), tf)

    heads = lambda a, b, t: a.reshape(b, t, N_HEADS, HEAD_DIM)
    return (yp.reshape(b_p, s_p, d), ys.reshape(b_s, t_new, d),
            heads(k_p, b_p, s_p), heads(v_p, b_p, s_p), lf_p[:, :N_HEADS].reshape(b_p, s_p, N_HEADS),
            tail_p[POOL_PAD - POOL_HIST:].reshape(b_p, POOL_HIST, d),
            heads(k_s, b_s, t_new), heads(v_s, b_s, t_new), lf_s[:, :N_HEADS].reshape(b_s, t_new, N_HEADS),
            hist_s[:, POOL_PAD - POOL_HIST:, :])
```

```python
import functools

import jax
import jax.numpy as jnp
from jax import lax
from jax.experimental import pallas as pl
from jax.experimental.pallas import tpu as pltpu

N_HEADS = 16
HEAD_DIM = 64
RMS_EPS = 1e-6
POOL_WINDOWS = (2, 4, 8, 16)
POOL_HIST = max(POOL_WINDOWS) - 1
POOL_PAD = POOL_HIST + 1
N_EXPERTS = 8
LANES = 128
N_SPLIT = 3
NEG_BIG = -1e30
VMEM_LIMIT_BYTES = 56 * 1024 * 1024

F32 = jnp.float32
BF16 = jnp.bfloat16
_NT = (((1,), (1,)), ((), ()))
_TN = (((0,), (0,)), ((), ()))


def _params(*sem):
    return pltpu.CompilerParams(dimension_semantics=sem, vmem_limit_bytes=VMEM_LIMIT_BYTES)


def _rmsnorm(x, gain):
    ms = jnp.mean(x * x, axis=-1, keepdims=True)
    return x * lax.rsqrt(ms + RMS_EPS) * gain


def _split3(x):
    hi = x.astype(BF16)
    r1 = x - hi.astype(F32)
    mid = r1.astype(BF16)
    lo = (r1 - mid.astype(F32)).astype(BF16)
    return hi, mid, lo


def _dot(a, b):
    return jnp.dot(a, b, preferred_element_type=F32)


def _dot_nt(a, b):
    return lax.dot_general(a, b, _NT, preferred_element_type=F32)


def _dot_exact_rhs(a_bf16, x_f32):
    hi, mid, lo = _split3(x_f32)
    return (_dot(a_bf16, hi) + _dot(a_bf16, mid)) + _dot(a_bf16, lo)


def _dot_exact_lhs(x_f32, a_bf16):
    hi, mid, lo = _split3(x_f32)
    return (_dot(hi, a_bf16) + _dot(mid, a_bf16)) + _dot(lo, a_bf16)


def _log_sigmoid(z):
    return jnp.minimum(z, 0.0) - jnp.log1p(jnp.exp(-jnp.abs(z)))


def _full_spec(shape):
    return pl.BlockSpec(shape, lambda *_: (0,) * len(shape))


def _qkvf_prompt_kernel(x_ref, g_ref, wq_ref, wk_ref, wv_ref, wf_ref, wfp_ref, bf_ref, bft_ref,
                        qt_ref, kh_ref, kt_ref, vt_ref, vtb_ref, lf_ref, lft_ref):
    xn = _rmsnorm(x_ref[...], g_ref[...]).astype(BF16)
    qt_ref[...] = (_dot_nt(wq_ref[...], xn) * (HEAD_DIM ** -0.5)).astype(BF16)
    kt_ref[...] = _dot_nt(wk_ref[...], xn)
    vt = _dot_nt(wv_ref[...], xn)
    vt_ref[...] = vt
    vtb_ref[...] = vt.astype(BF16)
    k = _dot_nt(xn, wk_ref[...])
    for h in range(N_HEADS):
        kh_ref[h] = k[:, h * HEAD_DIM:(h + 1) * HEAD_DIM].astype(BF16)
    lf_ref[...] = _log_sigmoid(_dot_nt(xn, wfp_ref[...]) + bf_ref[...])
    lft_ref[...] = _log_sigmoid(_dot_nt(wf_ref[...], xn) + bft_ref[...])


def _qkvf_prompt(x, gain, w_t, wf_pad, bf, bft, tm):
    n, d = x.shape
    feat_spec = pl.BlockSpec((d, tm), lambda i: (0, i))
    w_spec = lambda blk: pl.BlockSpec((d, d), lambda i: (blk, 0))
    feat = lambda dt: jax.ShapeDtypeStruct((d, n), dt)
    return pl.pallas_call(
        _qkvf_prompt_kernel,
        grid=(n // tm,),
        in_specs=[pl.BlockSpec((tm, d), lambda i: (i, 0)), _full_spec((1, d)),
                  w_spec(0), w_spec(1), w_spec(2),
                  pl.BlockSpec((N_HEADS, d), lambda i: (3 * d // N_HEADS, 0)),
                  _full_spec((LANES, d)), _full_spec((1, LANES)), _full_spec((N_HEADS, 1))],
        out_specs=[feat_spec, pl.BlockSpec((N_HEADS, tm, HEAD_DIM), lambda i: (0, i, 0)),
                   feat_spec, feat_spec, feat_spec,
                   pl.BlockSpec((tm, LANES), lambda i: (i, 0)),
                   pl.BlockSpec((N_HEADS, tm), lambda i: (0, i))],
        out_shape=[feat(BF16), jax.ShapeDtypeStruct((N_HEADS, n, HEAD_DIM), BF16),
                   feat(F32), feat(F32), feat(BF16),
                   jax.ShapeDtypeStruct((n, LANES), F32), jax.ShapeDtypeStruct((N_HEADS, n), F32)],
        compiler_params=_params("parallel"),
        name="qkvf_proj_prompt",
    )(x, gain, w_t, w_t, w_t, w_t, wf_pad, bf, bft)


def _qkvf_sample_kernel(x_ref, g_ref, wq_ref, wk_ref, wv_ref, wfp_ref, bf_ref, q_ref, k_ref, v_ref, lf_ref):
    xn = _rmsnorm(x_ref[...], g_ref[...]).astype(BF16)
    q_ref[...] = _dot_nt(xn, wq_ref[...]) * (HEAD_DIM ** -0.5)
    k_ref[...] = _dot_nt(xn, wk_ref[...])
    v_ref[...] = _dot_nt(xn, wv_ref[...])
    lf_ref[...] = _log_sigmoid(_dot_nt(xn, wfp_ref[...]) + bf_ref[...])


def _qkvf_sample(x, gain, w_t, wf_pad, bf, tm):
    n, d = x.shape
    row_spec = pl.BlockSpec((tm, d), lambda i: (i, 0))
    w_spec = lambda blk: pl.BlockSpec((d, d), lambda i: (blk, 0))
    rows = jax.ShapeDtypeStruct((n, d), F32)
    return pl.pallas_call(
        _qkvf_sample_kernel,
        grid=(n // tm,),
        in_specs=[row_spec, _full_spec((1, d)), w_spec(0), w_spec(1), w_spec(2),
                  _full_spec((LANES, d)), _full_spec((1, LANES))],
        out_specs=[row_spec, row_spec, row_spec, pl.BlockSpec((tm, LANES), lambda i: (i, 0))],
        out_shape=[rows, rows, rows, jax.ShapeDtypeStruct((n, LANES), F32)],
        compiler_params=_params("parallel"),
        name="qkvf_proj_sample",
    )(x, gain, w_t, w_t, w_t, wf_pad, bf)


def _key_aug_kernel(lf_ref, kh_ref, ka_ref, c_ref):
    @pl.when(pl.program_id(0) == 0)
    def _():
        c_ref[...] = jnp.zeros_like(c_ref)

    t = lf_ref.shape[0]
    row = lax.broadcasted_iota(jnp.int32, (t, t), 0)
    col = lax.broadcasted_iota(jnp.int32, (t, t), 1)
    fc = _dot_exact_rhs((col <= row).astype(BF16), lf_ref[...]) + c_ref[...]
    c_ref[...] = fc[t - 1:t, :]
    lane = lax.broadcasted_iota(jnp.int32, (t, HEAD_DIM), 1)
    for h in range(N_HEADS):
        terms = _split3(jnp.broadcast_to(-fc[:, h:h + 1], (t, HEAD_DIM)))
        aug = jnp.zeros((t, HEAD_DIM), F32)
        for i in range(N_SPLIT):
            aug = jnp.where(lane == i, terms[i].astype(F32), aug)
        ka_ref[h] = jnp.concatenate([kh_ref[h], aug.astype(BF16)], axis=1)


def _key_aug(lf, kh, t):
    s = lf.shape[0]
    return pl.pallas_call(
        _key_aug_kernel,
        grid=(s // t,),
        in_specs=[pl.BlockSpec((t, LANES), lambda i: (i, 0)),
                  pl.BlockSpec((N_HEADS, t, HEAD_DIM), lambda i: (0, i, 0))],
        out_specs=pl.BlockSpec((N_HEADS, t, 2 * HEAD_DIM), lambda i: (0, i, 0)),
        out_shape=jax.ShapeDtypeStruct((N_HEADS, s, 2 * HEAD_DIM), BF16),
        scratch_shapes=[pltpu.VMEM((1, LANES), F32)],
        compiler_params=_params("arbitrary"),
        name="key_aug_cumsum",
    )(lf, kh)


def _prompt_attn_kernel(qt_ref, ka_ref, vt_ref, o_ref, m_sc, l_sc, acc_sc, *, blk):
    qi = pl.program_id(1)
    ones_rows = (lax.broadcasted_iota(jnp.int32, (HEAD_DIM, blk), 0) < N_SPLIT).astype(BF16)
    qa = jnp.concatenate([qt_ref[...], ones_rows], axis=0)
    m_sc[...] = jnp.full_like(m_sc, NEG_BIG)
    l_sc[...] = jnp.zeros_like(l_sc)
    acc_sc[...] = jnp.zeros_like(acc_sc)

    def step(ki, causal):
        ks = pl.multiple_of(ki * blk, blk)
        s = _dot(ka_ref[0, pl.ds(ks, blk), :], qa)
        if causal:
            kidx = lax.broadcasted_iota(jnp.int32, (blk, blk), 0)
            qidx = lax.broadcasted_iota(jnp.int32, (blk, blk), 1)
            s = jnp.where(kidx <= qidx, s, NEG_BIG)
        m_prev = m_sc[...]
        m_new = jnp.maximum(m_prev, jnp.max(s, axis=0, keepdims=True))
        alpha = jnp.exp(m_prev - m_new)
        p = jnp.exp(s - m_new)
        l_sc[...] = alpha * l_sc[...] + jnp.sum(p, axis=0, keepdims=True)
        acc_sc[...] = alpha * acc_sc[...] + _dot(vt_ref[:, pl.ds(ks, blk)], p.astype(BF16))
        m_sc[...] = m_new

    def body(ki, carry):
        step(ki, False)
        return carry

    lax.fori_loop(0, qi, body, 0)
    step(qi, True)
    o_ref[...] = (acc_sc[...] / l_sc[...]).astype(o_ref.dtype)


def _prompt_attn(qt, ka, vt, blk):
    d, s = qt.shape
    return pl.pallas_call(
        functools.partial(_prompt_attn_kernel, blk=blk),
        grid=(N_HEADS, s // blk),
        in_specs=[pl.BlockSpec((HEAD_DIM, blk), lambda h, i: (h, i)),
                  pl.BlockSpec((1, s, 2 * HEAD_DIM), lambda h, i: (h, 0, 0)),
                  pl.BlockSpec((HEAD_DIM, s), lambda h, i: (h, 0))],
        out_specs=pl.BlockSpec((HEAD_DIM, blk), lambda h, i: (h, i)),
        out_shape=jax.ShapeDtypeStruct((d, s), BF16),
        scratch_shapes=[pltpu.VMEM((1, blk), F32), pltpu.VMEM((1, blk), F32),
                        pltpu.VMEM((HEAD_DIM, blk), F32)],
        compiler_params=_params("parallel", "parallel"),
        name="fox_prompt_attn",
    )(qt, ka, vt)


def _sample_attn_kernel(pt_ref, *refs, n_pp, page, t_new):
    del pt_ref
    k_refs = refs[:n_pp]
    v_refs = refs[n_pp:2 * n_pp]
    lf_refs = refs[2 * n_pp:3 * n_pp]
    q_ref, kn_ref, vn_ref, lfn_ref, o_ref, qbd_sc, m_sc, l_sc, acc_sc, suf_sc = refs[3 * n_pp:]
    j = pl.program_id(1)
    rows = N_HEADS * t_new
    d = N_HEADS * HEAD_DIM
    own = (lax.broadcasted_iota(jnp.int32, (rows, d), 1) // HEAD_DIM
           == lax.broadcasted_iota(jnp.int32, (rows, d), 0) // t_new)

    def online_update(s, pv_fn):
        m_prev = m_sc[...]
        m_new = jnp.maximum(m_prev, jnp.max(s, axis=1, keepdims=True))
        alpha = jnp.exp(m_prev - m_new)
        p = jnp.exp(s - m_new)
        l_sc[...] = alpha * l_sc[...] + jnp.sum(p, axis=1, keepdims=True)
        acc_sc[...] = alpha * acc_sc[...] + pv_fn(p.astype(BF16))
        m_sc[...] = m_new

    def add_head_rows(s, bias):
        return jnp.concatenate(
            [s[h * t_new:(h + 1) * t_new, :] + bias[h:h + 1, :] for h in range(N_HEADS)], axis=0)

    @pl.when(j == 0)
    def _():
        m_sc[...] = jnp.full_like(m_sc, NEG_BIG)
        l_sc[...] = jnp.zeros_like(l_sc)
        acc_sc[...] = jnp.zeros_like(acc_sc)
        suf_sc[...] = jnp.zeros_like(suf_sc)
        q_rep = jnp.concatenate([q_ref[...]] * N_HEADS, axis=0)
        qbd_sc[...] = jnp.where(own, q_rep, 0.0).astype(BF16)
        r8 = lax.broadcasted_iota(jnp.int32, (t_new, t_new), 0)
        c8 = lax.broadcasted_iota(jnp.int32, (t_new, t_new), 1)
        cum = _dot_exact_rhs((c8 <= r8).astype(BF16), lfn_ref[...])
        cum_pad = jnp.concatenate([cum, jnp.zeros((LANES - t_new, LANES), F32)], axis=0)
        eye = (lax.broadcasted_iota(jnp.int32, (N_HEADS, LANES), 0)
               == lax.broadcasted_iota(jnp.int32, (N_HEADS, LANES), 1)).astype(BF16)
        hi, mid, lo = _split3(cum_pad)
        cum_t = (_dot_nt(eye, hi) + _dot_nt(eye, mid)) + _dot_nt(eye, lo)
        zpad = jnp.zeros((LANES - t_new, d), F32)
        kn = jnp.concatenate([kn_ref[...], zpad], axis=0).astype(BF16)
        vn = jnp.concatenate([vn_ref[...], zpad], axis=0).astype(BF16)
        s = add_head_rows(_dot_nt(qbd_sc[...], kn), -cum_t)
        qt = lax.broadcasted_iota(jnp.int32, (rows, LANES), 0) % t_new
        kt = lax.broadcasted_iota(jnp.int32, (rows, LANES), 1)
        s = jnp.where(kt <= qt, s, NEG_BIG)
        online_update(s, lambda p: _dot(p, vn))

    rp = lax.broadcasted_iota(jnp.int32, (page, page), 0)
    cp = lax.broadcasted_iota(jnp.int32, (page, page), 1)
    newer = (rp > cp).astype(BF16)
    carry = suf_sc[...]
    suf_pages = [None] * n_pp
    for i in reversed(range(n_pp)):
        lf = lf_refs[i][...]
        suf_pages[i] = _dot_exact_lhs(lf, newer) + carry
        carry = carry + jnp.sum(lf, axis=1, keepdims=True)
    suf_sc[...] = carry
    suf = jnp.concatenate(suf_pages, axis=1)

    kt_pages = jnp.concatenate([r[...] for r in k_refs], axis=1).astype(BF16)
    vt_pages = jnp.concatenate([r[...] for r in v_refs], axis=1).astype(BF16)
    s = add_head_rows(_dot(qbd_sc[...], kt_pages), suf)
    online_update(s, lambda p: _dot_nt(p, vt_pages))

    @pl.when(j == pl.num_programs(1) - 1)
    def _():
        out = jnp.where(own, acc_sc[...] / l_sc[...], 0.0)
        o = out[0:t_new, :]
        for h in range(1, N_HEADS):
            o = o + out[h * t_new:(h + 1) * t_new, :]
        o_ref[...] = o


def _sample_attn(page_table, cache_kt, cache_vt, cache_lft, q_s, k_s, v_s, lf_s, n_pp):
    n_seq, n_pages = page_table.shape
    _, d, page = cache_kt.shape
    t_new = q_s.shape[0] // n_seq
    n_grp = n_pages // n_pp

    def page_idx(i):
        return lambda b, j, pt: (pt[b, (n_grp - 1 - j) * n_pp + i], 0, 0)

    kv_specs = [pl.BlockSpec((None, d, page), page_idx(i)) for i in range(n_pp)]
    lf_specs = [pl.BlockSpec((None, N_HEADS, page), page_idx(i)) for i in range(n_pp)]
    tok_spec = pl.BlockSpec((t_new, d), lambda b, j, pt: (b, 0))
    rows = N_HEADS * t_new
    grid_spec = pltpu.PrefetchScalarGridSpec(
        num_scalar_prefetch=1,
        grid=(n_seq, n_grp),
        in_specs=kv_specs + kv_specs + lf_specs + [
            tok_spec, tok_spec, tok_spec, pl.BlockSpec((t_new, LANES), lambda b, j, pt: (b, 0))],
        out_specs=tok_spec,
        scratch_shapes=[pltpu.VMEM((rows, d), BF16), pltpu.VMEM((rows, 1), F32), pltpu.VMEM((rows, 1), F32),
                        pltpu.VMEM((rows, d), F32), pltpu.VMEM((N_HEADS, 1), F32)],
    )
    return pl.pallas_call(
        functools.partial(_sample_attn_kernel, n_pp=n_pp, page=page, t_new=t_new),
        grid_spec=grid_spec,
        out_shape=jax.ShapeDtypeStruct((n_seq * t_new, d), F32),
        compiler_params=_params("parallel", "arbitrary"),
        name="fox_sample_attn",
    )(page_table, *([cache_kt] * n_pp), *([cache_vt] * n_pp), *([cache_lft] * n_pp), q_s, k_s, v_s, lf_s)


def _oproj_kernel(a_ref, x_ref, wo_ref, o_ref, *, feature_major):
    a = a_ref[...].astype(BF16)
    if feature_major:
        proj = lax.dot_general(a, wo_ref[...], _TN, preferred_element_type=F32)
    else:
        proj = _dot(a, wo_ref[...])
    o_ref[...] = x_ref[...] + proj


def _oproj(attn, x, wo, tm, feature_major):
    n, d = x.shape
    row_spec = pl.BlockSpec((tm, d), lambda i: (i, 0))
    attn_spec = pl.BlockSpec((d, tm), lambda i: (0, i)) if feature_major else row_spec
    return pl.pallas_call(
        functools.partial(_oproj_kernel, feature_major=feature_major),
        grid=(n // tm,),
        in_specs=[attn_spec, row_spec, _full_spec((d, d))],
        out_specs=row_spec,
        out_shape=jax.ShapeDtypeStruct((n, d), F32),
        compiler_params=_params("parallel"),
        name="attn_out_proj",
    )(attn, x, wo)


def _ffn_kernel(x_ref, g_ref, wg_ref, wu_ref, wd_ref, o_ref, xn_sc, acc_sc):
    j = pl.program_id(1)

    @pl.when(j == 0)
    def _():
        xn_sc[...] = _rmsnorm(x_ref[...], g_ref[...]).astype(BF16)
        acc_sc[...] = jnp.zeros_like(acc_sc)

    xn = xn_sc[...]
    gate = _dot(xn, wg_ref[...])
    up = _dot(xn, wu_ref[...])
    act = (gate * jax.nn.sigmoid(gate) * up).astype(BF16)
    acc_sc[...] += _dot(act, wd_ref[...])

    @pl.when(j == pl.num_programs(1) - 1)
    def _():
        o_ref[...] = x_ref[...] + acc_sc[...]


def _ffn(x, gain, w_gate_up, w_down, tm, tf):
    n, d = x.shape
    d_ff = w_down.shape[0]
    n_f = d_ff // tf
    row_spec = pl.BlockSpec((tm, d), lambda i, j: (i, 0))
    return pl.pallas_call(
        _ffn_kernel,
        grid=(n // tm, n_f),
        in_specs=[row_spec, pl.BlockSpec((1, d), lambda i, j: (0, 0)),
                  pl.BlockSpec((d, tf), lambda i, j: (0, j)),
                  pl.BlockSpec((d, tf), lambda i, j: (0, j + n_f)),
                  pl.BlockSpec((tf, d), lambda i, j: (j, 0))],
        out_specs=row_spec,
        out_shape=jax.ShapeDtypeStruct((n, d), F32),
        scratch_shapes=[pltpu.VMEM((tm, d), BF16), pltpu.VMEM((tm, d), F32)],
        compiler_params=_params("parallel", "arbitrary"),
        name="dense_swiglu",
    )(x, gain, w_gate_up, w_gate_up, w_down)


def _pool_mix(ext_ref, x, u, pos0, wgrp_ref, scale_ref, o_ref, n_rows):
    gdim = u.shape[1] // len(POOL_WINDOWS)
    pos = pos0 + lax.broadcasted_iota(jnp.int32, (n_rows, 1), 0)
    for g, w in enumerate(POOL_WINDOWS):
        cols = pl.ds(g * gdim, gdim)
        win = ext_ref[pl.ds(POOL_PAD, n_rows), cols]
        for i in range(1, w):
            win = win + ext_ref[pl.ds(POOL_PAD - i, n_rows), cols]
        count = jnp.minimum(pos + 1, w).astype(F32)
        pooled = (win / count - u[:, g * gdim:(g + 1) * gdim]).astype(BF16)
        mixed = _dot(pooled, wgrp_ref[g]) * scale_ref[:, cols]
        o_ref[:, cols] = x[:, g * gdim:(g + 1) * gdim] + mixed


def _pool_prompt_kernel(x_ref, g_ref, wgrp_ref, scale_ref, o_ref, tail_ref, ext_sc, *, tm):
    i = pl.program_id(0)

    @pl.when(i == 0)
    def _():
        ext_sc[pl.ds(0, POOL_PAD), :] = jnp.zeros((POOL_PAD, ext_sc.shape[1]), F32)

    @pl.when(i > 0)
    def _():
        ext_sc[pl.ds(0, POOL_PAD), :] = ext_sc[pl.ds(tm, POOL_PAD), :]

    x = x_ref[...]
    u = _rmsnorm(x, g_ref[...])
    ext_sc[pl.ds(POOL_PAD, tm), :] = u
    tail_ref[...] = u[tm - POOL_PAD:, :]
    _pool_mix(ext_sc, x, u, i * tm, wgrp_ref, scale_ref, o_ref, tm)


def _pool_prompt(x, gain, w_group, scale, tm):
    n, d = x.shape
    gdim = d // len(POOL_WINDOWS)
    row_spec = pl.BlockSpec((tm, d), lambda i: (i, 0))
    return pl.pallas_call(
        functools.partial(_pool_prompt_kernel, tm=tm),
        grid=(n // tm,),
        in_specs=[row_spec, pl.BlockSpec((1, d), lambda i: (0, 0)),
                  pl.BlockSpec((len(POOL_WINDOWS), gdim, gdim), lambda i: (0, 0, 0)),
                  pl.BlockSpec((1, d), lambda i: (0, 0))],
        out_specs=[row_spec, pl.BlockSpec((POOL_PAD, d), lambda i: (0, 0))],
        out_shape=[jax.ShapeDtypeStruct((n, d), F32), jax.ShapeDtypeStruct((POOL_PAD, d), F32)],
        scratch_shapes=[pltpu.VMEM((POOL_PAD + tm, d), F32)],
        compiler_params=_params("arbitrary"),
        name="pool_mix_prompt",
    )(x, gain, w_group, scale)


def _pool_sample_kernel(x_ref, st_ref, g_ref, wgrp_ref, scale_ref, o_ref, hist_ref, ext_sc, *, t_new, past_len):
    x = x_ref[...]
    u = _rmsnorm(x, g_ref[...])
    ext_sc[pl.ds(0, POOL_PAD), :] = st_ref[...]
    ext_sc[pl.ds(POOL_PAD, t_new), :] = u
    hist_ref[...] = ext_sc[pl.ds(t_new, POOL_PAD), :]
    _pool_mix(ext_sc, x, u, past_len, wgrp_ref, scale_ref, o_ref, t_new)


def _pool_sample(x, state_pad, gain, w_group, scale, t_new, past_len):
    n, d = x.shape
    gdim = d // len(POOL_WINDOWS)
    row_spec = pl.BlockSpec((t_new, d), lambda b: (b, 0))
    st_spec = pl.BlockSpec((None, POOL_PAD, d), lambda b: (b, 0, 0))
    return pl.pallas_call(
        functools.partial(_pool_sample_kernel, t_new=t_new, past_len=past_len),
        grid=(n // t_new,),
        in_specs=[row_spec, st_spec, pl.BlockSpec((1, d), lambda b: (0, 0)),
                  pl.BlockSpec((len(POOL_WINDOWS), gdim, gdim), lambda b: (0, 0, 0)),
                  pl.BlockSpec((1, d), lambda b: (0, 0))],
        out_specs=[row_spec, st_spec],
        out_shape=[jax.ShapeDtypeStruct((n, d), F32), jax.ShapeDtypeStruct(state_pad.shape, F32)],
        scratch_shapes=[pltpu.VMEM((POOL_PAD + t_new, d), F32)],
        compiler_params=_params("parallel"),
        name="pool_mix_sample",
    )(x, state_pad, gain, w_group, scale)


def _moe_kernel(x_ref, g_ref, wr_ref, wg_ref, wu_ref, wd_ref, gf_ref, o_ref, xn_sc, gates_sc, acc_sc):
    e = pl.program_id(1)
    j = pl.program_id(2)
    tm = x_ref.shape[0]
    lane = lax.broadcasted_iota(jnp.int32, (tm, LANES), 1)

    @pl.when((e == 0) & (j == 0))
    def _():
        xn = _rmsnorm(x_ref[...], g_ref[...])
        xh, xm, xl = _split3(xn)
        xn_sc[...] = xh
        wh, wm, wl = wr_ref[0], wr_ref[1], wr_ref[2]
        logits = ((_dot(xh, wl) + _dot(xm, wm) + _dot(xl, wh))
                  + (_dot(xh, wm) + _dot(xm, wh))) + _dot(xh, wh)
        logits = jnp.where(lane < N_EXPERTS, logits, -jnp.inf)
        top1 = jnp.max(logits, axis=1, keepdims=True)
        idx1 = jnp.min(jnp.where(logits == top1, lane, LANES), axis=1, keepdims=True)
        rest = jnp.where(lane == idx1, -jnp.inf, logits)
        top2 = jnp.max(rest, axis=1, keepdims=True)
        idx2 = jnp.min(jnp.where(rest == top2, lane, LANES), axis=1, keepdims=True)
        e2 = jnp.exp(top2 - top1)
        denom = 1.0 + e2
        gates_sc[...] = jnp.where(lane == idx1, 1.0 / denom, 0.0) + jnp.where(lane == idx2, e2 / denom, 0.0)
        acc_sc[...] = jnp.zeros_like(acc_sc)

    gate_e = jnp.sum(jnp.where(lane == e, gates_sc[...], 0.0), axis=1, keepdims=True)
    xn = xn_sc[...]
    gate = _dot(xn, wg_ref[...])
    up = _dot(xn, wu_ref[...])
    act = (gate * jax.nn.sigmoid(gate) * up * gate_e).astype(BF16)
    acc_sc[...] += _dot(act, wd_ref[...])

    @pl.when((e == pl.num_programs(1) - 1) & (j == pl.num_programs(2) - 1))
    def _():
        o_ref[...] = _rmsnorm(x_ref[...] + acc_sc[...], gf_ref[...])


def _moe(x, gain, w_router3, w_gate_up, w_down, gain_final, tm, tf):
    n, d = x.shape
    n_e, d_ff, _ = w_down.shape
    n_f = d_ff // tf
    row_spec = pl.BlockSpec((tm, d), lambda i, e, j: (i, 0))
    vec_spec = pl.BlockSpec((1, d), lambda i, e, j: (0, 0))
    return pl.pallas_call(
        _moe_kernel,
        grid=(n // tm, n_e, n_f),
        in_specs=[row_spec, vec_spec,
                  pl.BlockSpec((N_SPLIT, d, LANES), lambda i, e, j: (0, 0, 0)),
                  pl.BlockSpec((None, d, tf), lambda i, e, j: (e, 0, j)),
                  pl.BlockSpec((None, d, tf), lambda i, e, j: (e, 0, j + n_f)),
                  pl.BlockSpec((None, tf, d), lambda i, e, j: (e, j, 0)),
                  vec_spec],
        out_specs=row_spec,
        out_shape=jax.ShapeDtypeStruct((n, d), F32),
        scratch_shapes=[pltpu.VMEM((tm, d), BF16), pltpu.VMEM((tm, LANES), F32), pltpu.VMEM((tm, d), F32)],
        compiler_params=_params("parallel", "arbitrary", "arbitrary"),
        name="moe_swiglu_final_norm",
    )(x, gain, w_router3, w_gate_up, w_gate_up, w_down, gain_final)


def _row_tile(n, target):
    t = min(n, target)
    while n % t:
        t //= 2
    return t


def kernel(x_prompt, x_sample, cache_k, cache_v, cache_logf, state_pool, page_table, l0_norm_attn, l0_w_qkvf, l0_b_f, l0_w_o, l0_norm_ffn, l0_w_gate_up, l0_w_down, l1_norm_pool, l1_w_group, l1_pool_scale, l1_norm_ffn, l1_w_router, l1_w_gate_up, l1_w_down, final_norm):
    b_p, s_p, d = x_prompt.shape
    b_s, t_new, _ = x_sample.shape
    assert b_p == 1 and d == N_HEADS * HEAD_DIM
    n_phys, page = cache_k.shape[:2]
    past_len = page_table.shape[1] * page
    d_ff = l0_w_down.shape[0]

    row = lambda v: v.reshape(1, -1).astype(F32)
    w_t = l0_w_qkvf.T.astype(BF16)
    wf_pad = jnp.pad(w_t[3 * d:], ((0, LANES - N_HEADS), (0, 0)))
    bf = jnp.pad(l0_b_f, (0, LANES - N_HEADS)).reshape(1, LANES)
    bft = l0_b_f.reshape(N_HEADS, 1)
    wo = l0_w_o.astype(BF16)
    w0_gu = l0_w_gate_up.astype(BF16)
    w0_d = l0_w_down.astype(BF16)
    wgrp = l1_w_group.astype(BF16)
    wr_pad = jnp.pad(l1_w_router, ((0, 0), (0, LANES - N_EXPERTS)))
    wr_hi = wr_pad.astype(BF16)
    wr_r1 = wr_pad - wr_hi.astype(F32)
    wr_mid = wr_r1.astype(BF16)
    wr_lo = (wr_r1 - wr_mid.astype(F32)).astype(BF16)
    wr3 = jnp.stack([wr_hi, wr_mid, wr_lo])
    w1_gu = l1_w_gate_up.astype(BF16)
    w1_d = l1_w_down.astype(BF16)
    tf = 512 if d_ff % 512 == 0 else d_ff

    xp = x_prompt.reshape(s_p, d)
    xs = x_sample.reshape(b_s * t_new, d)
    n_s = b_s * t_new

    blk = _row_tile(s_p, 512)
    qt_p, kh_p, kt_p, vt_p, vtb_p, lf_p, lft_p = _qkvf_prompt(xp, row(l0_norm_attn), w_t, wf_pad, bf, bft, blk)
    q_s, k_s, v_s, lf_s = _qkvf_sample(xs, row(l0_norm_attn), w_t, wf_pad, bf, _row_tile(n_s, 512))

    attn_p = _prompt_attn(qt_p, _key_aug(lf_p, kh_p, blk), vtb_p, blk)

    feat_major = lambda c: jnp.transpose(c, (0, 2, 3, 1)).reshape(n_phys, d, page)
    n_pp = 8 if page_table.shape[1] % 8 == 0 else 1
    attn_s = _sample_attn(page_table, feat_major(cache_k), feat_major(cache_v), jnp.swapaxes(cache_logf, 1, 2),
                          q_s, k_s, v_s, lf_s, n_pp)

    hp = _oproj(attn_p, xp, wo, blk, True)
    hs = _oproj(attn_s, xs, wo, _row_tile(n_s, 512), False)

    hp = _ffn(hp, row(l0_norm_ffn), w0_gu, w0_d, _row_tile(s_p, 1024), tf)
    hs = _ffn(hs, row(l0_norm_ffn), w0_gu, w0_d, _row_tile(n_s, 1024), tf)

    hp, tail_p = _pool_prompt(hp, row(l1_norm_pool), wgrp, row(l1_pool_scale), _row_tile(s_p, 512))
    state_pad = jnp.pad(state_pool, ((0, 0), (POOL_PAD - POOL_HIST, 0), (0, 0)))
    hs, hist_s = _pool_sample(hs, state_pad, row(l1_norm_pool), wgrp, row(l1_pool_scale), t_new, past_len)

    yp = _moe(hp, row(l1_norm_ffn), wr3, w1_gu, w1_d, row(final_norm), _row_tile(s_p, 1024), tf)
    ys = _moe(hs, row(l1_norm_ffn), wr3, w1_gu, w1_d, row(final_norm), _row_tile(n_s, 1024), tf)

    heads_t = lambda a: jnp.transpose(a.reshape(1, N_HEADS, HEAD_DIM, s_p), (0, 3, 1, 2))
    heads = lambda a: a.reshape(b_s, t_new, N_HEADS, HEAD_DIM)
    return (yp.reshape(b_p, s_p, d), ys.reshape(b_s, t_new, d),
            heads_t(kt_p), heads_t(vt_p), jnp.transpose(lft_p.reshape(1, N_HEADS, s_p), (0, 2, 1)),
            tail_p[POOL_PAD - POOL_HIST:].reshape(b_p, POOL_HIST, d),
            heads(k_s), heads(v_s), lf_s[:, :N_HEADS].reshape(b_s, t_new, N_HEADS),
            hist_s[:, POOL_PAD - POOL_HIST:, :])
```

```python
import functools

import jax
import jax.numpy as jnp
from jax import lax
from jax.experimental import pallas as pl
from jax.experimental.pallas import tpu as pltpu

N_HEADS = 16
HEAD_DIM = 64
RMS_EPS = 1e-6
POOL_WINDOWS = (2, 4, 8, 16)
POOL_HIST = max(POOL_WINDOWS) - 1
POOL_PAD = POOL_HIST + 1
N_EXPERTS = 8
LANES = 128
N_SPLIT = 3
NEG_BIG = -1e30
LOG2_E = 1.4426950408889634
VMEM_LIMIT_BYTES = 56 * 1024 * 1024

F32 = jnp.float32
BF16 = jnp.bfloat16
_NT = (((1,), (1,)), ((), ()))
_TN = (((0,), (0,)), ((), ()))


def _params(*sem):
    return pltpu.CompilerParams(dimension_semantics=sem, vmem_limit_bytes=VMEM_LIMIT_BYTES)


def _rmsnorm(x, gain):
    ms = jnp.mean(x * x, axis=-1, keepdims=True)
    return x * lax.rsqrt(ms + RMS_EPS) * gain


def _split3(x):
    hi = x.astype(BF16)
    r1 = x - hi.astype(F32)
    mid = r1.astype(BF16)
    lo = (r1 - mid.astype(F32)).astype(BF16)
    return hi, mid, lo


def _dot(a, b):
    return jnp.dot(a, b, preferred_element_type=F32)


def _dot_nt(a, b):
    return lax.dot_general(a, b, _NT, preferred_element_type=F32)


def _dot_exact_rhs(a_bf16, x_f32):
    hi, mid, lo = _split3(x_f32)
    return (_dot(a_bf16, hi) + _dot(a_bf16, mid)) + _dot(a_bf16, lo)


def _dot_exact_lhs(x_f32, a_bf16):
    hi, mid, lo = _split3(x_f32)
    return (_dot(hi, a_bf16) + _dot(mid, a_bf16)) + _dot(lo, a_bf16)


def _log_sigmoid(z):
    return jnp.minimum(z, 0.0) - jnp.log1p(jnp.exp(-jnp.abs(z)))


def _full_spec(shape):
    return pl.BlockSpec(shape, lambda *_: (0,) * len(shape))


def _qkvf_prompt_kernel(x_ref, g_ref, wq_ref, wk_ref, wv_ref, wf_ref, wfp_ref, bf_ref, bft_ref,
                        qt_ref, kh_ref, kt_ref, vt_ref, vtb_ref, lf_ref, lft_ref):
    xn = _rmsnorm(x_ref[...], g_ref[...]).astype(BF16)
    qt_ref[...] = (_dot_nt(wq_ref[...], xn) * (LOG2_E * HEAD_DIM ** -0.5)).astype(BF16)
    kt_ref[...] = _dot_nt(wk_ref[...], xn)
    vt = _dot_nt(wv_ref[...], xn)
    vt_ref[...] = vt
    vtb_ref[...] = vt.astype(BF16)
    k = _dot_nt(xn, wk_ref[...])
    for h in range(N_HEADS):
        kh_ref[h] = k[:, h * HEAD_DIM:(h + 1) * HEAD_DIM].astype(BF16)
    lf_ref[...] = _log_sigmoid(_dot_nt(xn, wfp_ref[...]) + bf_ref[...])
    lft_ref[...] = _log_sigmoid(_dot_nt(wf_ref[...], xn) + bft_ref[...])


def _qkvf_prompt(x, gain, w_t, wf_pad, bf, bft, tm):
    n, d = x.shape
    feat_spec = pl.BlockSpec((d, tm), lambda i: (0, i))
    w_spec = lambda blk: pl.BlockSpec((d, d), lambda i: (blk, 0))
    feat = lambda dt: jax.ShapeDtypeStruct((d, n), dt)
    return pl.pallas_call(
        _qkvf_prompt_kernel,
        grid=(n // tm,),
        in_specs=[pl.BlockSpec((tm, d), lambda i: (i, 0)), _full_spec((1, d)),
                  w_spec(0), w_spec(1), w_spec(2),
                  pl.BlockSpec((N_HEADS, d), lambda i: (3 * d // N_HEADS, 0)),
                  _full_spec((LANES, d)), _full_spec((1, LANES)), _full_spec((N_HEADS, 1))],
        out_specs=[feat_spec, pl.BlockSpec((N_HEADS, tm, HEAD_DIM), lambda i: (0, i, 0)),
                   feat_spec, feat_spec, feat_spec,
                   pl.BlockSpec((tm, LANES), lambda i: (i, 0)),
                   pl.BlockSpec((N_HEADS, tm), lambda i: (0, i))],
        out_shape=[feat(BF16), jax.ShapeDtypeStruct((N_HEADS, n, HEAD_DIM), BF16),
                   feat(F32), feat(F32), feat(BF16),
                   jax.ShapeDtypeStruct((n, LANES), F32), jax.ShapeDtypeStruct((N_HEADS, n), F32)],
        compiler_params=_params("parallel"),
        name="qkvf_proj_prompt",
    )(x, gain, w_t, w_t, w_t, w_t, wf_pad, bf, bft)


def _qkvf_sample_kernel(x_ref, g_ref, wq_ref, wk_ref, wv_ref, wfp_ref, bf_ref, q_ref, k_ref, v_ref, lf_ref):
    xn = _rmsnorm(x_ref[...], g_ref[...]).astype(BF16)
    q_ref[...] = _dot_nt(xn, wq_ref[...]) * (HEAD_DIM ** -0.5)
    k_ref[...] = _dot_nt(xn, wk_ref[...])
    v_ref[...] = _dot_nt(xn, wv_ref[...])
    lf_ref[...] = _log_sigmoid(_dot_nt(xn, wfp_ref[...]) + bf_ref[...])


def _qkvf_sample(x, gain, w_t, wf_pad, bf, tm):
    n, d = x.shape
    row_spec = pl.BlockSpec((tm, d), lambda i: (i, 0))
    w_spec = lambda blk: pl.BlockSpec((d, d), lambda i: (blk, 0))
    rows = jax.ShapeDtypeStruct((n, d), F32)
    return pl.pallas_call(
        _qkvf_sample_kernel,
        grid=(n // tm,),
        in_specs=[row_spec, _full_spec((1, d)), w_spec(0), w_spec(1), w_spec(2),
                  _full_spec((LANES, d)), _full_spec((1, LANES))],
        out_specs=[row_spec, row_spec, row_spec, pl.BlockSpec((tm, LANES), lambda i: (i, 0))],
        out_shape=[rows, rows, rows, jax.ShapeDtypeStruct((n, LANES), F32)],
        compiler_params=_params("parallel"),
        name="qkvf_proj_sample",
    )(x, gain, w_t, w_t, w_t, wf_pad, bf)


def _key_aug_kernel(lf_ref, kh_ref, ka_ref, c_ref):
    @pl.when(pl.program_id(0) == 0)
    def _():
        c_ref[...] = jnp.zeros_like(c_ref)

    t = lf_ref.shape[0]
    row = lax.broadcasted_iota(jnp.int32, (t, t), 0)
    col = lax.broadcasted_iota(jnp.int32, (t, t), 1)
    fc = _dot_exact_rhs((col <= row).astype(BF16), lf_ref[...]) + c_ref[...]
    c_ref[...] = fc[t - 1:t, :]
    lane = lax.broadcasted_iota(jnp.int32, (t, HEAD_DIM), 1)
    for h in range(N_HEADS):
        terms = _split3(jnp.broadcast_to(-LOG2_E * fc[:, h:h + 1], (t, HEAD_DIM)))
        aug = jnp.zeros((t, HEAD_DIM), F32)
        for i in range(N_SPLIT):
            aug = jnp.where(lane == i, terms[i].astype(F32), aug)
        ka_ref[h] = jnp.concatenate([kh_ref[h], aug.astype(BF16)], axis=1)


def _key_aug(lf, kh, t):
    s = lf.shape[0]
    return pl.pallas_call(
        _key_aug_kernel,
        grid=(s // t,),
        in_specs=[pl.BlockSpec((t, LANES), lambda i: (i, 0)),
                  pl.BlockSpec((N_HEADS, t, HEAD_DIM), lambda i: (0, i, 0))],
        out_specs=pl.BlockSpec((N_HEADS, t, 2 * HEAD_DIM), lambda i: (0, i, 0)),
        out_shape=jax.ShapeDtypeStruct((N_HEADS, s, 2 * HEAD_DIM), BF16),
        scratch_shapes=[pltpu.VMEM((1, LANES), F32)],
        compiler_params=_params("arbitrary"),
        name="key_aug_cumsum",
    )(lf, kh)


def _prompt_attn_kernel(qt_ref, ka_ref, vt_ref, o_ref, s_sc, mc_sc, m_sc, l_sc, acc_sc, *, blk):
    qi = pl.program_id(1)
    ones_rows = (lax.broadcasted_iota(jnp.int32, (HEAD_DIM, blk), 0) < N_SPLIT).astype(BF16)
    qa = jnp.concatenate([qt_ref[...], ones_rows], axis=0)
    m_sc[...] = jnp.full_like(m_sc, NEG_BIG)
    l_sc[...] = jnp.zeros_like(l_sc)
    acc_sc[...] = jnp.zeros_like(acc_sc)
    kidx = lax.broadcasted_iota(jnp.int32, (blk, blk), 0)
    qidx = lax.broadcasted_iota(jnp.int32, (blk, blk), 1)

    def scores(ki, slot, causal):
        ks = pl.multiple_of(ki * blk, blk)
        s = _dot(ka_ref[0, pl.ds(ks, blk), :], qa)
        if causal:
            s = jnp.where(kidx <= qidx, s, NEG_BIG)
        s_sc[slot] = s
        mc_sc[slot] = jnp.max(s, axis=0, keepdims=True)

    def softmax_pv(ki, slot):
        ks = pl.multiple_of(ki * blk, blk)
        m_prev = m_sc[...]
        m_new = jnp.maximum(m_prev, mc_sc[slot])
        alpha = jnp.exp2(m_prev - m_new)
        p = jnp.exp2(s_sc[slot] - m_new)
        l_sc[...] = alpha * l_sc[...] + jnp.sum(p, axis=0, keepdims=True)
        acc_sc[...] = alpha * acc_sc[...] + _dot(vt_ref[:, pl.ds(ks, blk)], p.astype(BF16))
        m_sc[...] = m_new

    @pl.when(qi == 0)
    def _():
        scores(0, 0, True)
        softmax_pv(0, 0)

    @pl.when(qi > 0)
    def _():
        scores(0, 0, False)
        n_pairs = (qi - 1) // 2

        def body(k, carry):
            scores(2 * k + 1, 1, False)
            softmax_pv(2 * k, 0)
            scores(2 * k + 2, 0, False)
            softmax_pv(2 * k + 1, 1)
            return carry

        lax.fori_loop(0, n_pairs, body, 0)
        i0 = 2 * n_pairs

        @pl.when(qi - i0 == 2)
        def _():
            scores(i0 + 1, 1, False)
            softmax_pv(i0, 0)
            scores(qi, 0, True)
            softmax_pv(i0 + 1, 1)
            softmax_pv(qi, 0)

        @pl.when(qi - i0 == 1)
        def _():
            scores(qi, 1, True)
            softmax_pv(i0, 0)
            softmax_pv(qi, 1)

    o_ref[...] = (acc_sc[...] / l_sc[...]).astype(o_ref.dtype)


def _prompt_attn(qt, ka, vt, blk):
    d, s = qt.shape
    return pl.pallas_call(
        functools.partial(_prompt_attn_kernel, blk=blk),
        grid=(N_HEADS, s // blk),
        in_specs=[pl.BlockSpec((HEAD_DIM, blk), lambda h, i: (h, i)),
                  pl.BlockSpec((1, s, 2 * HEAD_DIM), lambda h, i: (h, 0, 0)),
                  pl.BlockSpec((HEAD_DIM, s), lambda h, i: (h, 0))],
        out_specs=pl.BlockSpec((HEAD_DIM, blk), lambda h, i: (h, i)),
        out_shape=jax.ShapeDtypeStruct((d, s), BF16),
        scratch_shapes=[pltpu.VMEM((2, blk, blk), F32), pltpu.VMEM((2, 1, blk), F32),
                        pltpu.VMEM((1, blk), F32), pltpu.VMEM((1, blk), F32),
                        pltpu.VMEM((HEAD_DIM, blk), F32)],
        compiler_params=_params("parallel", "parallel"),
        name="fox_prompt_attn",
    )(qt, ka, vt)


def _sample_attn_kernel(pt_ref, *refs, n_pp, page, t_new):
    del pt_ref
    k_refs = refs[:n_pp]
    v_refs = refs[n_pp:2 * n_pp]
    lf_refs = refs[2 * n_pp:3 * n_pp]
    q_ref, kn_ref, vn_ref, lfn_ref, o_ref, qbd_sc, m_sc, l_sc, acc_sc, suf_sc = refs[3 * n_pp:]
    j = pl.program_id(1)
    rows = N_HEADS * t_new
    d = N_HEADS * HEAD_DIM
    own = (lax.broadcasted_iota(jnp.int32, (rows, d), 1) // HEAD_DIM
           == lax.broadcasted_iota(jnp.int32, (rows, d), 0) // t_new)

    def online_update(s, pv_fn):
        m_prev = m_sc[...]
        m_new = jnp.maximum(m_prev, jnp.max(s, axis=1, keepdims=True))
        alpha = jnp.exp(m_prev - m_new)
        p = jnp.exp(s - m_new)
        l_sc[...] = alpha * l_sc[...] + jnp.sum(p, axis=1, keepdims=True)
        acc_sc[...] = alpha * acc_sc[...] + pv_fn(p.astype(BF16))
        m_sc[...] = m_new

    def add_head_rows(s, bias):
        return jnp.concatenate(
            [s[h * t_new:(h + 1) * t_new, :] + bias[h:h + 1, :] for h in range(N_HEADS)], axis=0)

    @pl.when(j == 0)
    def _():
        m_sc[...] = jnp.full_like(m_sc, NEG_BIG)
        l_sc[...] = jnp.zeros_like(l_sc)
        acc_sc[...] = jnp.zeros_like(acc_sc)
        suf_sc[...] = jnp.zeros_like(suf_sc)
        q_rep = jnp.concatenate([q_ref[...]] * N_HEADS, axis=0)
        qbd_sc[...] = jnp.where(own, q_rep, 0.0).astype(BF16)
        r8 = lax.broadcasted_iota(jnp.int32, (t_new, t_new), 0)
        c8 = lax.broadcasted_iota(jnp.int32, (t_new, t_new), 1)
        cum = _dot_exact_rhs((c8 <= r8).astype(BF16), lfn_ref[...])
        cum_pad = jnp.concatenate([cum, jnp.zeros((LANES - t_new, LANES), F32)], axis=0)
        eye = (lax.broadcasted_iota(jnp.int32, (N_HEADS, LANES), 0)
               == lax.broadcasted_iota(jnp.int32, (N_HEADS, LANES), 1)).astype(BF16)
        hi, mid, lo = _split3(cum_pad)
        cum_t = (_dot_nt(eye, hi) + _dot_nt(eye, mid)) + _dot_nt(eye, lo)
        zpad = jnp.zeros((LANES - t_new, d), F32)
        kn = jnp.concatenate([kn_ref[...], zpad], axis=0).astype(BF16)
        vn = jnp.concatenate([vn_ref[...], zpad], axis=0).astype(BF16)
        s = add_head_rows(_dot_nt(qbd_sc[...], kn), -cum_t)
        qt = lax.broadcasted_iota(jnp.int32, (rows, LANES), 0) % t_new
        kt = lax.broadcasted_iota(jnp.int32, (rows, LANES), 1)
        s = jnp.where(kt <= qt, s, NEG_BIG)
        online_update(s, lambda p: _dot(p, vn))

    rp = lax.broadcasted_iota(jnp.int32, (page, page), 0)
    cp = lax.broadcasted_iota(jnp.int32, (page, page), 1)
    newer = (rp > cp).astype(BF16)
    carry = suf_sc[...]
    suf_pages = [None] * n_pp
    for i in reversed(range(n_pp)):
        lf = lf_refs[i][...]
        suf_pages[i] = _dot_exact_lhs(lf, newer) + carry
        carry = carry + jnp.sum(lf, axis=1, keepdims=True)
    suf_sc[...] = carry
    suf = jnp.concatenate(suf_pages, axis=1)

    kt_pages = jnp.concatenate([r[...] for r in k_refs], axis=1).astype(BF16)
    vt_pages = jnp.concatenate([r[...] for r in v_refs], axis=1).astype(BF16)
    s = add_head_rows(_dot(qbd_sc[...], kt_pages), suf)
    online_update(s, lambda p: _dot_nt(p, vt_pages))

    @pl.when(j == pl.num_programs(1) - 1)
    def _():
        out = jnp.where(own, acc_sc[...] / l_sc[...], 0.0)
        o = out[0:t_new, :]
        for h in range(1, N_HEADS):
            o = o + out[h * t_new:(h + 1) * t_new, :]
        o_ref[...] = o


def _sample_attn(page_table, cache_kt, cache_vt, cache_lft, q_s, k_s, v_s, lf_s, n_pp):
    n_seq, n_pages = page_table.shape
    _, d, page = cache_kt.shape
    t_new = q_s.shape[0] // n_seq
    n_grp = n_pages // n_pp

    def page_idx(i):
        return lambda b, j, pt: (pt[b, (n_grp - 1 - j) * n_pp + i], 0, 0)

    kv_specs = [pl.BlockSpec((None, d, page), page_idx(i)) for i in range(n_pp)]
    lf_specs = [pl.BlockSpec((None, N_HEADS, page), page_idx(i)) for i in range(n_pp)]
    tok_spec = pl.BlockSpec((t_new, d), lambda b, j, pt: (b, 0))
    rows = N_HEADS * t_new
    grid_spec = pltpu.PrefetchScalarGridSpec(
        num_scalar_prefetch=1,
        grid=(n_seq, n_grp),
        in_specs=kv_specs + kv_specs + lf_specs + [
            tok_spec, tok_spec, tok_spec, pl.BlockSpec((t_new, LANES), lambda b, j, pt: (b, 0))],
        out_specs=tok_spec,
        scratch_shapes=[pltpu.VMEM((rows, d), BF16), pltpu.VMEM((rows, 1), F32), pltpu.VMEM((rows, 1), F32),
                        pltpu.VMEM((rows, d), F32), pltpu.VMEM((N_HEADS, 1), F32)],
    )
    return pl.pallas_call(
        functools.partial(_sample_attn_kernel, n_pp=n_pp, page=page, t_new=t_new),
        grid_spec=grid_spec,
        out_shape=jax.ShapeDtypeStruct((n_seq * t_new, d), F32),
        compiler_params=_params("parallel", "arbitrary"),
        name="fox_sample_attn",
    )(page_table, *([cache_kt] * n_pp), *([cache_vt] * n_pp), *([cache_lft] * n_pp), q_s, k_s, v_s, lf_s)


def _oproj_kernel(a_ref, x_ref, wo_ref, o_ref, *, feature_major):
    a = a_ref[...].astype(BF16)
    if feature_major:
        proj = lax.dot_general(a, wo_ref[...], _TN, preferred_element_type=F32)
    else:
        proj = _dot(a, wo_ref[...])
    o_ref[...] = x_ref[...] + proj


def _oproj(attn, x, wo, tm, feature_major):
    n, d = x.shape
    row_spec = pl.BlockSpec((tm, d), lambda i: (i, 0))
    attn_spec = pl.BlockSpec((d, tm), lambda i: (0, i)) if feature_major else row_spec
    return pl.pallas_call(
        functools.partial(_oproj_kernel, feature_major=feature_major),
        grid=(n // tm,),
        in_specs=[attn_spec, row_spec, _full_spec((d, d))],
        out_specs=row_spec,
        out_shape=jax.ShapeDtypeStruct((n, d), F32),
        compiler_params=_params("parallel"),
        name="attn_out_proj",
    )(attn, x, wo)


def _ffn_kernel(x_ref, g_ref, wg_ref, wu_ref, wd_ref, o_ref, xn_sc, acc_sc):
    j = pl.program_id(1)

    @pl.when(j == 0)
    def _():
        xn_sc[...] = _rmsnorm(x_ref[...], g_ref[...]).astype(BF16)
        acc_sc[...] = jnp.zeros_like(acc_sc)

    xn = xn_sc[...]
    gate = _dot(xn, wg_ref[...])
    up = _dot(xn, wu_ref[...])
    act = (gate * jax.nn.sigmoid(gate) * up).astype(BF16)
    acc_sc[...] += _dot(act, wd_ref[...])

    @pl.when(j == pl.num_programs(1) - 1)
    def _():
        o_ref[...] = x_ref[...] + acc_sc[...]


def _ffn(x, gain, w_gate_up, w_down, tm, tf):
    n, d = x.shape
    d_ff = w_down.shape[0]
    n_f = d_ff // tf
    row_spec = pl.BlockSpec((tm, d), lambda i, j: (i, 0))
    return pl.pallas_call(
        _ffn_kernel,
        grid=(n // tm, n_f),
        in_specs=[row_spec, pl.BlockSpec((1, d), lambda i, j: (0, 0)),
                  pl.BlockSpec((d, tf), lambda i, j: (0, j)),
                  pl.BlockSpec((d, tf), lambda i, j: (0, j + n_f)),
                  pl.BlockSpec((tf, d), lambda i, j: (j, 0))],
        out_specs=row_spec,
        out_shape=jax.ShapeDtypeStruct((n, d), F32),
        scratch_shapes=[pltpu.VMEM((tm, d), BF16), pltpu.VMEM((tm, d), F32)],
        compiler_params=_params("parallel", "arbitrary"),
        name="dense_swiglu",
    )(x, gain, w_gate_up, w_gate_up, w_down)


def _pool_mix(ext_ref, x, u, pos0, wgrp_ref, scale_ref, o_ref, n_rows):
    gdim = u.shape[1] // len(POOL_WINDOWS)
    pos = pos0 + lax.broadcasted_iota(jnp.int32, (n_rows, 1), 0)
    for g, w in enumerate(POOL_WINDOWS):
        cols = pl.ds(g * gdim, gdim)
        win = ext_ref[pl.ds(POOL_PAD, n_rows), cols]
        for i in range(1, w):
            win = win + ext_ref[pl.ds(POOL_PAD - i, n_rows), cols]
        count = jnp.minimum(pos + 1, w).astype(F32)
        pooled = (win / count - u[:, g * gdim:(g + 1) * gdim]).astype(BF16)
        mixed = _dot(pooled, wgrp_ref[g]) * scale_ref[:, cols]
        o_ref[:, cols] = x[:, g * gdim:(g + 1) * gdim] + mixed


def _pool_prompt_kernel(x_ref, g_ref, wgrp_ref, scale_ref, o_ref, tail_ref, ext_sc, *, tm):
    i = pl.program_id(0)

    @pl.when(i == 0)
    def _():
        ext_sc[pl.ds(0, POOL_PAD), :] = jnp.zeros((POOL_PAD, ext_sc.shape[1]), F32)

    @pl.when(i > 0)
    def _():
        ext_sc[pl.ds(0, POOL_PAD), :] = ext_sc[pl.ds(tm, POOL_PAD), :]

    x = x_ref[...]
    u = _rmsnorm(x, g_ref[...])
    ext_sc[pl.ds(POOL_PAD, tm), :] = u
    tail_ref[...] = u[tm - POOL_PAD:, :]
    _pool_mix(ext_sc, x, u, i * tm, wgrp_ref, scale_ref, o_ref, tm)


def _pool_prompt(x, gain, w_group, scale, tm):
    n, d = x.shape
    gdim = d // len(POOL_WINDOWS)
    row_spec = pl.BlockSpec((tm, d), lambda i: (i, 0))
    return pl.pallas_call(
        functools.partial(_pool_prompt_kernel, tm=tm),
        grid=(n // tm,),
        in_specs=[row_spec, pl.BlockSpec((1, d), lambda i: (0, 0)),
                  pl.BlockSpec((len(POOL_WINDOWS), gdim, gdim), lambda i: (0, 0, 0)),
                  pl.BlockSpec((1, d), lambda i: (0, 0))],
        out_specs=[row_spec, pl.BlockSpec((POOL_PAD, d), lambda i: (0, 0))],
        out_shape=[jax.ShapeDtypeStruct((n, d), F32), jax.ShapeDtypeStruct((POOL_PAD, d), F32)],
        scratch_shapes=[pltpu.VMEM((POOL_PAD + tm, d), F32)],
        compiler_params=_params("arbitrary"),
        name="pool_mix_prompt",
    )(x, gain, w_group, scale)


def _pool_sample_kernel(x_ref, st_ref, g_ref, wgrp_ref, scale_ref, o_ref, hist_ref, ext_sc, *, t_new, past_len):
    x = x_ref[...]
    u = _rmsnorm(x, g_ref[...])
    ext_sc[pl.ds(0, POOL_PAD), :] = st_ref[...]
    ext_sc[pl.ds(POOL_PAD, t_new), :] = u
    hist_ref[...] = ext_sc[pl.ds(t_new, POOL_PAD), :]
    _pool_mix(ext_sc, x, u, past_len, wgrp_ref, scale_ref, o_ref, t_new)


def _pool_sample(x, state_pad, gain, w_group, scale, t_new, past_len):
    n, d = x.shape
    gdim = d // len(POOL_WINDOWS)
    row_spec = pl.BlockSpec((t_new, d), lambda b: (b, 0))
    st_spec = pl.BlockSpec((None, POOL_PAD, d), lambda b: (b, 0, 0))
    return pl.pallas_call(
        functools.partial(_pool_sample_kernel, t_new=t_new, past_len=past_len),
        grid=(n // t_new,),
        in_specs=[row_spec, st_spec, pl.BlockSpec((1, d), lambda b: (0, 0)),
                  pl.BlockSpec((len(POOL_WINDOWS), gdim, gdim), lambda b: (0, 0, 0)),
                  pl.BlockSpec((1, d), lambda b: (0, 0))],
        out_specs=[row_spec, st_spec],
        out_shape=[jax.ShapeDtypeStruct((n, d), F32), jax.ShapeDtypeStruct(state_pad.shape, F32)],
        scratch_shapes=[pltpu.VMEM((POOL_PAD + t_new, d), F32)],
        compiler_params=_params("parallel"),
        name="pool_mix_sample",
    )(x, state_pad, gain, w_group, scale)


def _moe_kernel(x_ref, g_ref, wr_ref, wg_ref, wu_ref, wd_ref, gf_ref, o_ref, xn_sc, gates_sc, acc_sc):
    e = pl.program_id(1)
    j = pl.program_id(2)
    tm = x_ref.shape[0]
    lane = lax.broadcasted_iota(jnp.int32, (tm, LANES), 1)

    @pl.when((e == 0) & (j == 0))
    def _():
        xn = _rmsnorm(x_ref[...], g_ref[...])
        xh, xm, xl = _split3(xn)
        xn_sc[...] = xh
        wh, wm, wl = wr_ref[0], wr_ref[1], wr_ref[2]
        logits = ((_dot(xh, wl) + _dot(xm, wm) + _dot(xl, wh))
                  + (_dot(xh, wm) + _dot(xm, wh))) + _dot(xh, wh)
        logits = jnp.where(lane < N_EXPERTS, logits, -jnp.inf)
        top1 = jnp.max(logits, axis=1, keepdims=True)
        idx1 = jnp.min(jnp.where(logits == top1, lane, LANES), axis=1, keepdims=True)
        rest = jnp.where(lane == idx1, -jnp.inf, logits)
        top2 = jnp.max(rest, axis=1, keepdims=True)
        idx2 = jnp.min(jnp.where(rest == top2, lane, LANES), axis=1, keepdims=True)
        e2 = jnp.exp(top2 - top1)
        denom = 1.0 + e2
        gates_sc[...] = jnp.where(lane == idx1, 1.0 / denom, 0.0) + jnp.where(lane == idx2, e2 / denom, 0.0)
        acc_sc[...] = jnp.zeros_like(acc_sc)

    gate_e = jnp.sum(jnp.where(lane == e, gates_sc[...], 0.0), axis=1, keepdims=True)
    xn = xn_sc[...]
    gate = _dot(xn, wg_ref[...])
    up = _dot(xn, wu_ref[...])
    act = (gate * jax.nn.sigmoid(gate) * up * gate_e).astype(BF16)
    acc_sc[...] += _dot(act, wd_ref[...])

    @pl.when((e == pl.num_programs(1) - 1) & (j == pl.num_programs(2) - 1))
    def _():
        o_ref[...] = _rmsnorm(x_ref[...] + acc_sc[...], gf_ref[...])


def _moe(x, gain, w_router3, w_gate_up, w_down, gain_final, tm, tf):
    n, d = x.shape
    n_e, d_ff, _ = w_down.shape
    n_f = d_ff // tf
    row_spec = pl.BlockSpec((tm, d), lambda i, e, j: (i, 0))
    vec_spec = pl.BlockSpec((1, d), lambda i, e, j: (0, 0))
    return pl.pallas_call(
        _moe_kernel,
        grid=(n // tm, n_e, n_f),
        in_specs=[row_spec, vec_spec,
                  pl.BlockSpec((N_SPLIT, d, LANES), lambda i, e, j: (0, 0, 0)),
                  pl.BlockSpec((None, d, tf), lambda i, e, j: (e, 0, j)),
                  pl.BlockSpec((None, d, tf), lambda i, e, j: (e, 0, j + n_f)),
                  pl.BlockSpec((None, tf, d), lambda i, e, j: (e, j, 0)),
                  vec_spec],
        out_specs=row_spec,
        out_shape=jax.ShapeDtypeStruct((n, d), F32),
        scratch_shapes=[pltpu.VMEM((tm, d), BF16), pltpu.VMEM((tm, LANES), F32), pltpu.VMEM((tm, d), F32)],
        compiler_params=_params("parallel", "arbitrary", "arbitrary"),
        name="moe_swiglu_final_norm",
    )(x, gain, w_router3, w_gate_up, w_gate_up, w_down, gain_final)


def _row_tile(n, target):
    t = min(n, target)
    while n % t:
        t //= 2
    return t


def kernel(x_prompt, x_sample, cache_k, cache_v, cache_logf, state_pool, page_table, l0_norm_attn, l0_w_qkvf, l0_b_f, l0_w_o, l0_norm_ffn, l0_w_gate_up, l0_w_down, l1_norm_pool, l1_w_group, l1_pool_scale, l1_norm_ffn, l1_w_router, l1_w_gate_up, l1_w_down, final_norm):
    b_p, s_p, d = x_prompt.shape
    b_s, t_new, _ = x_sample.shape
    assert b_p == 1 and d == N_HEADS * HEAD_DIM
    n_phys, page = cache_k.shape[:2]
    past_len = page_table.shape[1] * page
    d_ff = l0_w_down.shape[0]

    row = lambda v: v.reshape(1, -1).astype(F32)
    w_t = l0_w_qkvf.T.astype(BF16)
    wf_pad = jnp.pad(w_t[3 * d:], ((0, LANES - N_HEADS), (0, 0)))
    bf = jnp.pad(l0_b_f, (0, LANES - N_HEADS)).reshape(1, LANES)
    bft = l0_b_f.reshape(N_HEADS, 1)
    wo = l0_w_o.astype(BF16)
    w0_gu = l0_w_gate_up.astype(BF16)
    w0_d = l0_w_down.astype(BF16)
    wgrp = l1_w_group.astype(BF16)
    wr_pad = jnp.pad(l1_w_router, ((0, 0), (0, LANES - N_EXPERTS)))
    wr_hi = wr_pad.astype(BF16)
    wr_r1 = wr_pad - wr_hi.astype(F32)
    wr_mid = wr_r1.astype(BF16)
    wr_lo = (wr_r1 - wr_mid.astype(F32)).astype(BF16)
    wr3 = jnp.stack([wr_hi, wr_mid, wr_lo])
    w1_gu = l1_w_gate_up.astype(BF16)
    w1_d = l1_w_down.astype(BF16)
    tf = 512 if d_ff % 512 == 0 else d_ff

    xp = x_prompt.reshape(s_p, d)
    xs = x_sample.reshape(b_s * t_new, d)
    n_s = b_s * t_new

    blk = _row_tile(s_p, 512)
    qt_p, kh_p, kt_p, vt_p, vtb_p, lf_p, lft_p = _qkvf_prompt(xp, row(l0_norm_attn), w_t, wf_pad, bf, bft, blk)
    q_s, k_s, v_s, lf_s = _qkvf_sample(xs, row(l0_norm_attn), w_t, wf_pad, bf, _row_tile(n_s, 512))

    attn_p = _prompt_attn(qt_p, _key_aug(lf_p, kh_p, blk), vtb_p, blk)

    feat_major = lambda c: jnp.transpose(c, (0, 2, 3, 1)).reshape(n_phys, d, page)
    n_pp = 8 if page_table.shape[1] % 8 == 0 else 1
    attn_s = _sample_attn(page_table, feat_major(cache_k), feat_major(cache_v), jnp.swapaxes(cache_logf, 1, 2),
                          q_s, k_s, v_s, lf_s, n_pp)

    hp = _oproj(attn_p, xp, wo, blk, True)
    hs = _oproj(attn_s, xs, wo, _row_tile(n_s, 512), False)

    hp = _ffn(hp, row(l0_norm_ffn), w0_gu, w0_d, _row_tile(s_p, 1024), tf)
    hs = _ffn(hs, row(l0_norm_ffn), w0_gu, w0_d, _row_tile(n_s, 1024), tf)

    hp, tail_p = _pool_prompt(hp, row(l1_norm_pool), wgrp, row(l1_pool_scale), _row_tile(s_p, 512))
    state_pad = jnp.pad(state_pool, ((0, 0), (POOL_PAD - POOL_HIST, 0), (0, 0)))
    hs, hist_s = _pool_sample(hs, state_pad, row(l1_norm_pool), wgrp, row(l1_pool_scale), t_new, past_len)

    yp = _moe(hp, row(l1_norm_ffn), wr3, w1_gu, w1_d, row(final_norm), _row_tile(s_p, 1024), tf)
    ys = _moe(hs, row(l1_norm_ffn), wr3, w1_gu, w1_d, row(final_norm), _row_tile(n_s, 1024), tf)

    heads_t = lambda a: jnp.transpose(a.reshape(1, N_HEADS, HEAD_DIM, s_p), (0, 3, 1, 2))
    heads = lambda a: a.reshape(b_s, t_new, N_HEADS, HEAD_DIM)
    return (yp.reshape(b_p, s_p, d), ys.reshape(b_s, t_new, d),
            heads_t(kt_p), heads_t(vt_p), jnp.transpose(lft_p.reshape(1, N_HEADS, s_p), (0, 2, 1)),
            tail_p[POOL_PAD - POOL_HIST:].reshape(b_p, POOL_HIST, d),
            heads(k_s), heads(v_s), lf_s[:, :N_HEADS].reshape(b_s, t_new, N_HEADS),
            hist_s[:, POOL_PAD - POOL_HIST:, :])
```

```python
import functools

import jax
import jax.numpy as jnp
from jax import lax
from jax.experimental import pallas as pl
from jax.experimental.pallas import tpu as pltpu

N_HEADS = 16
HEAD_DIM = 64
RMS_EPS = 1e-6
POOL_WINDOWS = (2, 4, 8, 16)
POOL_HIST = max(POOL_WINDOWS) - 1
POOL_PAD = POOL_HIST + 1
N_EXPERTS = 8
LANES = 128
N_SPLIT = 3
NEG_BIG = -1e30
LOG2_E = 1.4426950408889634
MOE_CHUNK = 128
VMEM_LIMIT_BYTES = 56 * 1024 * 1024

F32 = jnp.float32
BF16 = jnp.bfloat16
_NT = (((1,), (1,)), ((), ()))
_TN = (((0,), (0,)), ((), ()))


def _params(*sem):
    return pltpu.CompilerParams(dimension_semantics=sem, vmem_limit_bytes=VMEM_LIMIT_BYTES)


def _rmsnorm(x, gain):
    ms = jnp.mean(x * x, axis=-1, keepdims=True)
    return x * lax.rsqrt(ms + RMS_EPS) * gain


def _split3(x):
    hi = x.astype(BF16)
    r1 = x - hi.astype(F32)
    mid = r1.astype(BF16)
    lo = (r1 - mid.astype(F32)).astype(BF16)
    return hi, mid, lo


def _dot(a, b):
    return jnp.dot(a, b, preferred_element_type=F32)


def _dot_nt(a, b):
    return lax.dot_general(a, b, _NT, preferred_element_type=F32)


def _dot_exact_rhs(a_bf16, x_f32):
    hi, mid, lo = _split3(x_f32)
    return (_dot(a_bf16, hi) + _dot(a_bf16, mid)) + _dot(a_bf16, lo)


def _dot_exact_lhs(x_f32, a_bf16):
    hi, mid, lo = _split3(x_f32)
    return (_dot(hi, a_bf16) + _dot(mid, a_bf16)) + _dot(lo, a_bf16)


def _log_sigmoid(z):
    return jnp.minimum(z, 0.0) - jnp.log1p(jnp.exp(-jnp.abs(z)))


def _full_spec(shape):
    return pl.BlockSpec(shape, lambda *_: (0,) * len(shape))


def _qkvf_prompt_kernel(x_ref, g_ref, wq_ref, wk_ref, wv_ref, wf_ref, wfp_ref, bf_ref, bft_ref,
                        qt_ref, kh_ref, kt_ref, vt_ref, vtb_ref, lf_ref, lft_ref):
    xn = _rmsnorm(x_ref[...], g_ref[...]).astype(BF16)
    qt_ref[...] = (_dot_nt(wq_ref[...], xn) * (LOG2_E * HEAD_DIM ** -0.5)).astype(BF16)
    kt_ref[...] = _dot_nt(wk_ref[...], xn)
    vt = _dot_nt(wv_ref[...], xn)
    vt_ref[...] = vt
    vtb_ref[...] = vt.astype(BF16)
    k = _dot_nt(xn, wk_ref[...])
    for h in range(N_HEADS):
        kh_ref[h] = k[:, h * HEAD_DIM:(h + 1) * HEAD_DIM].astype(BF16)
    lf_ref[...] = _log_sigmoid(_dot_nt(xn, wfp_ref[...]) + bf_ref[...])
    lft_ref[...] = _log_sigmoid(_dot_nt(wf_ref[...], xn) + bft_ref[...])


def _qkvf_prompt(x, gain, w_t, wf_pad, bf, bft, tm):
    n, d = x.shape
    feat_spec = pl.BlockSpec((d, tm), lambda i: (0, i))
    w_spec = lambda blk: pl.BlockSpec((d, d), lambda i: (blk, 0))
    feat = lambda dt: jax.ShapeDtypeStruct((d, n), dt)
    return pl.pallas_call(
        _qkvf_prompt_kernel,
        grid=(n // tm,),
        in_specs=[pl.BlockSpec((tm, d), lambda i: (i, 0)), _full_spec((1, d)),
                  w_spec(0), w_spec(1), w_spec(2),
                  pl.BlockSpec((N_HEADS, d), lambda i: (3 * d // N_HEADS, 0)),
                  _full_spec((LANES, d)), _full_spec((1, LANES)), _full_spec((N_HEADS, 1))],
        out_specs=[feat_spec, pl.BlockSpec((N_HEADS, tm, HEAD_DIM), lambda i: (0, i, 0)),
                   feat_spec, feat_spec, feat_spec,
                   pl.BlockSpec((tm, LANES), lambda i: (i, 0)),
                   pl.BlockSpec((N_HEADS, tm), lambda i: (0, i))],
        out_shape=[feat(BF16), jax.ShapeDtypeStruct((N_HEADS, n, HEAD_DIM), BF16),
                   feat(F32), feat(F32), feat(BF16),
                   jax.ShapeDtypeStruct((n, LANES), F32), jax.ShapeDtypeStruct((N_HEADS, n), F32)],
        compiler_params=_params("parallel"),
        name="qkvf_proj_prompt",
    )(x, gain, w_t, w_t, w_t, w_t, wf_pad, bf, bft)


def _qkvf_sample_kernel(x_ref, g_ref, wq_ref, wk_ref, wv_ref, wfp_ref, bf_ref, q_ref, k_ref, v_ref, lf_ref):
    xn = _rmsnorm(x_ref[...], g_ref[...]).astype(BF16)
    q_ref[...] = _dot_nt(xn, wq_ref[...]) * (HEAD_DIM ** -0.5)
    k_ref[...] = _dot_nt(xn, wk_ref[...])
    v_ref[...] = _dot_nt(xn, wv_ref[...])
    lf_ref[...] = _log_sigmoid(_dot_nt(xn, wfp_ref[...]) + bf_ref[...])


def _qkvf_sample(x, gain, w_t, wf_pad, bf, tm):
    n, d = x.shape
    row_spec = pl.BlockSpec((tm, d), lambda i: (i, 0))
    w_spec = lambda blk: pl.BlockSpec((d, d), lambda i: (blk, 0))
    rows = jax.ShapeDtypeStruct((n, d), F32)
    return pl.pallas_call(
        _qkvf_sample_kernel,
        grid=(n // tm,),
        in_specs=[row_spec, _full_spec((1, d)), w_spec(0), w_spec(1), w_spec(2),
                  _full_spec((LANES, d)), _full_spec((1, LANES))],
        out_specs=[row_spec, row_spec, row_spec, pl.BlockSpec((tm, LANES), lambda i: (i, 0))],
        out_shape=[rows, rows, rows, jax.ShapeDtypeStruct((n, LANES), F32)],
        compiler_params=_params("parallel"),
        name="qkvf_proj_sample",
    )(x, gain, w_t, w_t, w_t, wf_pad, bf)


def _key_aug_kernel(lf_ref, kh_ref, ka_ref, c_ref):
    @pl.when(pl.program_id(0) == 0)
    def _():
        c_ref[...] = jnp.zeros_like(c_ref)

    t = lf_ref.shape[0]
    row = lax.broadcasted_iota(jnp.int32, (t, t), 0)
    col = lax.broadcasted_iota(jnp.int32, (t, t), 1)
    fc = _dot_exact_rhs((col <= row).astype(BF16), lf_ref[...]) + c_ref[...]
    c_ref[...] = fc[t - 1:t, :]
    lane = lax.broadcasted_iota(jnp.int32, (t, HEAD_DIM), 1)
    for h in range(N_HEADS):
        terms = _split3(jnp.broadcast_to(-LOG2_E * fc[:, h:h + 1], (t, HEAD_DIM)))
        aug = jnp.zeros((t, HEAD_DIM), F32)
        for i in range(N_SPLIT):
            aug = jnp.where(lane == i, terms[i].astype(F32), aug)
        ka_ref[h] = jnp.concatenate([kh_ref[h], aug.astype(BF16)], axis=1)


def _key_aug(lf, kh, t):
    s = lf.shape[0]
    return pl.pallas_call(
        _key_aug_kernel,
        grid=(s // t,),
        in_specs=[pl.BlockSpec((t, LANES), lambda i: (i, 0)),
                  pl.BlockSpec((N_HEADS, t, HEAD_DIM), lambda i: (0, i, 0))],
        out_specs=pl.BlockSpec((N_HEADS, t, 2 * HEAD_DIM), lambda i: (0, i, 0)),
        out_shape=jax.ShapeDtypeStruct((N_HEADS, s, 2 * HEAD_DIM), BF16),
        scratch_shapes=[pltpu.VMEM((1, LANES), F32)],
        compiler_params=_params("arbitrary"),
        name="key_aug_cumsum",
    )(lf, kh)


def _prompt_attn_kernel(qt_ref, ka_ref, vt_ref, o_ref, s_sc, mc_sc, m_sc, l_sc, acc_sc, *, blk):
    qi = pl.program_id(1)
    ones_rows = (lax.broadcasted_iota(jnp.int32, (HEAD_DIM, blk), 0) < N_SPLIT).astype(BF16)
    qa = jnp.concatenate([qt_ref[...], ones_rows], axis=0)
    m_sc[...] = jnp.full_like(m_sc, NEG_BIG)
    l_sc[...] = jnp.zeros_like(l_sc)
    acc_sc[...] = jnp.zeros_like(acc_sc)
    kidx = lax.broadcasted_iota(jnp.int32, (blk, blk), 0)
    qidx = lax.broadcasted_iota(jnp.int32, (blk, blk), 1)

    def scores(ki, slot, causal):
        ks = pl.multiple_of(ki * blk, blk)
        s = _dot(ka_ref[0, pl.ds(ks, blk), :], qa)
        if causal:
            s = jnp.where(kidx <= qidx, s, NEG_BIG)
        s_sc[slot] = s
        mc_sc[slot] = jnp.max(s, axis=0, keepdims=True)

    def softmax_pv(ki, slot):
        ks = pl.multiple_of(ki * blk, blk)
        m_prev = m_sc[...]
        m_new = jnp.maximum(m_prev, mc_sc[slot])
        alpha = jnp.exp2(m_prev - m_new)
        p = jnp.exp2(s_sc[slot] - m_new)
        l_sc[...] = alpha * l_sc[...] + jnp.sum(p, axis=0, keepdims=True)
        acc_sc[...] = alpha * acc_sc[...] + _dot(vt_ref[:, pl.ds(ks, blk)], p.astype(BF16))
        m_sc[...] = m_new

    @pl.when(qi == 0)
    def _():
        scores(0, 0, True)
        softmax_pv(0, 0)

    @pl.when(qi > 0)
    def _():
        scores(0, 0, False)
        n_pairs = (qi - 1) // 2

        def body(k, carry):
            scores(2 * k + 1, 1, False)
            softmax_pv(2 * k, 0)
            scores(2 * k + 2, 0, False)
            softmax_pv(2 * k + 1, 1)
            return carry

        lax.fori_loop(0, n_pairs, body, 0)
        i0 = 2 * n_pairs

        @pl.when(qi - i0 == 2)
        def _():
            scores(i0 + 1, 1, False)
            softmax_pv(i0, 0)
            scores(qi, 0, True)
            softmax_pv(i0 + 1, 1)
            softmax_pv(qi, 0)

        @pl.when(qi - i0 == 1)
        def _():
            scores(qi, 1, True)
            softmax_pv(i0, 0)
            softmax_pv(qi, 1)

    o_ref[...] = (acc_sc[...] / l_sc[...]).astype(o_ref.dtype)


def _prompt_attn(qt, ka, vt, blk):
    d, s = qt.shape
    return pl.pallas_call(
        functools.partial(_prompt_attn_kernel, blk=blk),
        grid=(N_HEADS, s // blk),
        in_specs=[pl.BlockSpec((HEAD_DIM, blk), lambda h, i: (h, i)),
                  pl.BlockSpec((1, s, 2 * HEAD_DIM), lambda h, i: (h, 0, 0)),
                  pl.BlockSpec((HEAD_DIM, s), lambda h, i: (h, 0))],
        out_specs=pl.BlockSpec((HEAD_DIM, blk), lambda h, i: (h, i)),
        out_shape=jax.ShapeDtypeStruct((d, s), BF16),
        scratch_shapes=[pltpu.VMEM((2, blk, blk), F32), pltpu.VMEM((2, 1, blk), F32),
                        pltpu.VMEM((1, blk), F32), pltpu.VMEM((1, blk), F32),
                        pltpu.VMEM((HEAD_DIM, blk), F32)],
        compiler_params=_params("parallel", "parallel"),
        name="fox_prompt_attn",
    )(qt, ka, vt)


def _sample_attn_kernel(pt_ref, *refs, n_pp, page, t_new):
    del pt_ref
    k_refs = refs[:n_pp]
    v_refs = refs[n_pp:2 * n_pp]
    lf_refs = refs[2 * n_pp:3 * n_pp]
    q_ref, kn_ref, vn_ref, lfn_ref, o_ref, qbd_sc, m_sc, l_sc, acc_sc, suf_sc = refs[3 * n_pp:]
    j = pl.program_id(1)
    rows = N_HEADS * t_new
    d = N_HEADS * HEAD_DIM
    own = (lax.broadcasted_iota(jnp.int32, (rows, d), 1) // HEAD_DIM
           == lax.broadcasted_iota(jnp.int32, (rows, d), 0) // t_new)

    def online_update(s, pv_fn):
        m_prev = m_sc[...]
        m_new = jnp.maximum(m_prev, jnp.max(s, axis=1, keepdims=True))
        alpha = jnp.exp(m_prev - m_new)
        p = jnp.exp(s - m_new)
        l_sc[...] = alpha * l_sc[...] + jnp.sum(p, axis=1, keepdims=True)
        acc_sc[...] = alpha * acc_sc[...] + pv_fn(p.astype(BF16))
        m_sc[...] = m_new

    def add_head_rows(s, bias):
        return jnp.concatenate(
            [s[h * t_new:(h + 1) * t_new, :] + bias[h:h + 1, :] for h in range(N_HEADS)], axis=0)

    @pl.when(j == 0)
    def _():
        m_sc[...] = jnp.full_like(m_sc, NEG_BIG)
        l_sc[...] = jnp.zeros_like(l_sc)
        acc_sc[...] = jnp.zeros_like(acc_sc)
        suf_sc[...] = jnp.zeros_like(suf_sc)
        q_rep = jnp.concatenate([q_ref[...]] * N_HEADS, axis=0)
        qbd_sc[...] = jnp.where(own, q_rep, 0.0).astype(BF16)
        r8 = lax.broadcasted_iota(jnp.int32, (t_new, t_new), 0)
        c8 = lax.broadcasted_iota(jnp.int32, (t_new, t_new), 1)
        cum = _dot_exact_rhs((c8 <= r8).astype(BF16), lfn_ref[...])
        cum_pad = jnp.concatenate([cum, jnp.zeros((LANES - t_new, LANES), F32)], axis=0)
        eye = (lax.broadcasted_iota(jnp.int32, (N_HEADS, LANES), 0)
               == lax.broadcasted_iota(jnp.int32, (N_HEADS, LANES), 1)).astype(BF16)
        hi, mid, lo = _split3(cum_pad)
        cum_t = (_dot_nt(eye, hi) + _dot_nt(eye, mid)) + _dot_nt(eye, lo)
        zpad = jnp.zeros((LANES - t_new, d), F32)
        kn = jnp.concatenate([kn_ref[...], zpad], axis=0).astype(BF16)
        vn = jnp.concatenate([vn_ref[...], zpad], axis=0).astype(BF16)
        s = add_head_rows(_dot_nt(qbd_sc[...], kn), -cum_t)
        qt = lax.broadcasted_iota(jnp.int32, (rows, LANES), 0) % t_new
        kt = lax.broadcasted_iota(jnp.int32, (rows, LANES), 1)
        s = jnp.where(kt <= qt, s, NEG_BIG)
        online_update(s, lambda p: _dot(p, vn))

    rp = lax.broadcasted_iota(jnp.int32, (page, page), 0)
    cp = lax.broadcasted_iota(jnp.int32, (page, page), 1)
    newer = (rp > cp).astype(BF16)
    carry = suf_sc[...]
    suf_pages = [None] * n_pp
    for i in reversed(range(n_pp)):
        lf = lf_refs[i][...]
        suf_pages[i] = _dot_exact_lhs(lf, newer) + carry
        carry = carry + jnp.sum(lf, axis=1, keepdims=True)
    suf_sc[...] = carry
    suf = jnp.concatenate(suf_pages, axis=1)

    kt_pages = jnp.concatenate([r[...] for r in k_refs], axis=1).astype(BF16)
    vt_pages = jnp.concatenate([r[...] for r in v_refs], axis=1).astype(BF16)
    s = add_head_rows(_dot(qbd_sc[...], kt_pages), suf)
    online_update(s, lambda p: _dot_nt(p, vt_pages))

    @pl.when(j == pl.num_programs(1) - 1)
    def _():
        out = jnp.where(own, acc_sc[...] / l_sc[...], 0.0)
        o = out[0:t_new, :]
        for h in range(1, N_HEADS):
            o = o + out[h * t_new:(h + 1) * t_new, :]
        o_ref[...] = o


def _sample_attn(page_table, cache_kt, cache_vt, cache_lft, q_s, k_s, v_s, lf_s, n_pp):
    n_seq, n_pages = page_table.shape
    _, d, page = cache_kt.shape
    t_new = q_s.shape[0] // n_seq
    n_grp = n_pages // n_pp

    def page_idx(i):
        return lambda b, j, pt: (pt[b, (n_grp - 1 - j) * n_pp + i], 0, 0)

    kv_specs = [pl.BlockSpec((None, d, page), page_idx(i)) for i in range(n_pp)]
    lf_specs = [pl.BlockSpec((None, N_HEADS, page), page_idx(i)) for i in range(n_pp)]
    tok_spec = pl.BlockSpec((t_new, d), lambda b, j, pt: (b, 0))
    rows = N_HEADS * t_new
    grid_spec = pltpu.PrefetchScalarGridSpec(
        num_scalar_prefetch=1,
        grid=(n_seq, n_grp),
        in_specs=kv_specs + kv_specs + lf_specs + [
            tok_spec, tok_spec, tok_spec, pl.BlockSpec((t_new, LANES), lambda b, j, pt: (b, 0))],
        out_specs=tok_spec,
        scratch_shapes=[pltpu.VMEM((rows, d), BF16), pltpu.VMEM((rows, 1), F32), pltpu.VMEM((rows, 1), F32),
                        pltpu.VMEM((rows, d), F32), pltpu.VMEM((N_HEADS, 1), F32)],
    )
    return pl.pallas_call(
        functools.partial(_sample_attn_kernel, n_pp=n_pp, page=page, t_new=t_new),
        grid_spec=grid_spec,
        out_shape=jax.ShapeDtypeStruct((n_seq * t_new, d), F32),
        compiler_params=_params("parallel", "arbitrary"),
        name="fox_sample_attn",
    )(page_table, *([cache_kt] * n_pp), *([cache_vt] * n_pp), *([cache_lft] * n_pp), q_s, k_s, v_s, lf_s)


def _oproj_kernel(a_ref, x_ref, wo_ref, o_ref, *, feature_major):
    a = a_ref[...].astype(BF16)
    if feature_major:
        proj = lax.dot_general(a, wo_ref[...], _TN, preferred_element_type=F32)
    else:
        proj = _dot(a, wo_ref[...])
    o_ref[...] = x_ref[...] + proj


def _oproj(attn, x, wo, tm, feature_major):
    n, d = x.shape
    row_spec = pl.BlockSpec((tm, d), lambda i: (i, 0))
    attn_spec = pl.BlockSpec((d, tm), lambda i: (0, i)) if feature_major else row_spec
    return pl.pallas_call(
        functools.partial(_oproj_kernel, feature_major=feature_major),
        grid=(n // tm,),
        in_specs=[attn_spec, row_spec, _full_spec((d, d))],
        out_specs=row_spec,
        out_shape=jax.ShapeDtypeStruct((n, d), F32),
        compiler_params=_params("parallel"),
        name="attn_out_proj",
    )(attn, x, wo)


def _ffn_kernel(x_ref, g_ref, wg_ref, wu_ref, wd_ref, o_ref, xn_sc, acc_sc):
    j = pl.program_id(1)

    @pl.when(j == 0)
    def _():
        xn_sc[...] = _rmsnorm(x_ref[...], g_ref[...]).astype(BF16)
        acc_sc[...] = jnp.zeros_like(acc_sc)

    xn = xn_sc[...]
    gate = _dot(xn, wg_ref[...])
    up = _dot(xn, wu_ref[...])
    act = (gate * jax.nn.sigmoid(gate) * up).astype(BF16)
    acc_sc[...] += _dot(act, wd_ref[...])

    @pl.when(j == pl.num_programs(1) - 1)
    def _():
        o_ref[...] = x_ref[...] + acc_sc[...]


def _ffn(x, gain, w_gate_up, w_down, tm, tf):
    n, d = x.shape
    d_ff = w_down.shape[0]
    n_f = d_ff // tf
    row_spec = pl.BlockSpec((tm, d), lambda i, j: (i, 0))
    return pl.pallas_call(
        _ffn_kernel,
        grid=(n // tm, n_f),
        in_specs=[row_spec, pl.BlockSpec((1, d), lambda i, j: (0, 0)),
                  pl.BlockSpec((d, tf), lambda i, j: (0, j)),
                  pl.BlockSpec((d, tf), lambda i, j: (0, j + n_f)),
                  pl.BlockSpec((tf, d), lambda i, j: (j, 0))],
        out_specs=row_spec,
        out_shape=jax.ShapeDtypeStruct((n, d), F32),
        scratch_shapes=[pltpu.VMEM((tm, d), BF16), pltpu.VMEM((tm, d), F32)],
        compiler_params=_params("parallel", "arbitrary"),
        name="dense_swiglu",
    )(x, gain, w_gate_up, w_gate_up, w_down)


def _pool_mix(ext_ref, x, u, pos0, wgrp_ref, scale_ref, o_ref, n_rows):
    gdim = u.shape[1] // len(POOL_WINDOWS)
    pos = pos0 + lax.broadcasted_iota(jnp.int32, (n_rows, 1), 0)
    for g, w in enumerate(POOL_WINDOWS):
        cols = pl.ds(g * gdim, gdim)
        win = ext_ref[pl.ds(POOL_PAD, n_rows), cols]
        for i in range(1, w):
            win = win + ext_ref[pl.ds(POOL_PAD - i, n_rows), cols]
        count = jnp.minimum(pos + 1, w).astype(F32)
        pooled = (win / count - u[:, g * gdim:(g + 1) * gdim]).astype(BF16)
        mixed = _dot(pooled, wgrp_ref[g]) * scale_ref[:, cols]
        o_ref[:, cols] = x[:, g * gdim:(g + 1) * gdim] + mixed


def _pool_prompt_kernel(x_ref, g_ref, wgrp_ref, scale_ref, o_ref, tail_ref, ext_sc, *, tm):
    i = pl.program_id(0)

    @pl.when(i == 0)
    def _():
        ext_sc[pl.ds(0, POOL_PAD), :] = jnp.zeros((POOL_PAD, ext_sc.shape[1]), F32)

    @pl.when(i > 0)
    def _():
        ext_sc[pl.ds(0, POOL_PAD), :] = ext_sc[pl.ds(tm, POOL_PAD), :]

    x = x_ref[...]
    u = _rmsnorm(x, g_ref[...])
    ext_sc[pl.ds(POOL_PAD, tm), :] = u
    tail_ref[...] = u[tm - POOL_PAD:, :]
    _pool_mix(ext_sc, x, u, i * tm, wgrp_ref, scale_ref, o_ref, tm)


def _pool_prompt(x, gain, w_group, scale, tm):
    n, d = x.shape
    gdim = d // len(POOL_WINDOWS)
    row_spec = pl.BlockSpec((tm, d), lambda i: (i, 0))
    return pl.pallas_call(
        functools.partial(_pool_prompt_kernel, tm=tm),
        grid=(n // tm,),
        in_specs=[row_spec, pl.BlockSpec((1, d), lambda i: (0, 0)),
                  pl.BlockSpec((len(POOL_WINDOWS), gdim, gdim), lambda i: (0, 0, 0)),
                  pl.BlockSpec((1, d), lambda i: (0, 0))],
        out_specs=[row_spec, pl.BlockSpec((POOL_PAD, d), lambda i: (0, 0))],
        out_shape=[jax.ShapeDtypeStruct((n, d), F32), jax.ShapeDtypeStruct((POOL_PAD, d), F32)],
        scratch_shapes=[pltpu.VMEM((POOL_PAD + tm, d), F32)],
        compiler_params=_params("arbitrary"),
        name="pool_mix_prompt",
    )(x, gain, w_group, scale)


def _pool_sample_kernel(x_ref, st_ref, g_ref, wgrp_ref, scale_ref, o_ref, hist_ref, ext_sc, *, t_new, past_len):
    x = x_ref[...]
    u = _rmsnorm(x, g_ref[...])
    ext_sc[pl.ds(0, POOL_PAD), :] = st_ref[...]
    ext_sc[pl.ds(POOL_PAD, t_new), :] = u
    hist_ref[...] = ext_sc[pl.ds(t_new, POOL_PAD), :]
    _pool_mix(ext_sc, x, u, past_len, wgrp_ref, scale_ref, o_ref, t_new)


def _pool_sample(x, state_pad, gain, w_group, scale, t_new, past_len):
    n, d = x.shape
    gdim = d // len(POOL_WINDOWS)
    row_spec = pl.BlockSpec((t_new, d), lambda b: (b, 0))
    st_spec = pl.BlockSpec((None, POOL_PAD, d), lambda b: (b, 0, 0))
    return pl.pallas_call(
        functools.partial(_pool_sample_kernel, t_new=t_new, past_len=past_len),
        grid=(n // t_new,),
        in_specs=[row_spec, st_spec, pl.BlockSpec((1, d), lambda b: (0, 0)),
                  pl.BlockSpec((len(POOL_WINDOWS), gdim, gdim), lambda b: (0, 0, 0)),
                  pl.BlockSpec((1, d), lambda b: (0, 0))],
        out_specs=[row_spec, st_spec],
        out_shape=[jax.ShapeDtypeStruct((n, d), F32), jax.ShapeDtypeStruct(state_pad.shape, F32)],
        scratch_shapes=[pltpu.VMEM((POOL_PAD + t_new, d), F32)],
        compiler_params=_params("parallel"),
        name="pool_mix_sample",
    )(x, state_pad, gain, w_group, scale)


def _moe_kernel(x_ref, g_ref, wrt_ref, wg_ref, wu_ref, wd_ref, gf_ref, o_ref,
                xn_sc, gate_sc, rank_sc, xe_sc, ge_sc, ye_sc, acc_sc, *, c0):
    e = pl.program_id(1)
    j = pl.program_id(2)
    tm = x_ref.shape[0]
    e_rows = gate_sc.shape[0]

    @pl.when((e == 0) & (j == 0))
    def _():
        xn = _rmsnorm(x_ref[...], g_ref[...])
        xh, xm, xl = _split3(xn)
        xn_sc[...] = xh
        wh, wm, wl = wrt_ref[0], wrt_ref[1], wrt_ref[2]
        logits = ((_dot_nt(wl, xh) + _dot_nt(wm, xm) + _dot_nt(wh, xl))
                  + (_dot_nt(wm, xh) + _dot_nt(wh, xm))) + _dot_nt(wh, xh)
        erow = lax.broadcasted_iota(jnp.int32, (e_rows, tm), 0)
        logits = jnp.where(erow < N_EXPERTS, logits, -jnp.inf)
        top1 = jnp.max(logits, axis=0, keepdims=True)
        idx1 = jnp.min(jnp.where(logits == top1, erow, e_rows), axis=0, keepdims=True)
        rest = jnp.where(erow == idx1, -jnp.inf, logits)
        top2 = jnp.max(rest, axis=0, keepdims=True)
        idx2 = jnp.min(jnp.where(rest == top2, erow, e_rows), axis=0, keepdims=True)
        e2 = jnp.exp(top2 - top1)
        denom = 1.0 + e2
        gate_sc[...] = jnp.where(erow == idx1, 1.0 / denom, 0.0) + jnp.where(erow == idx2, e2 / denom, 0.0)
        routed = (erow == idx1) | (erow == idx2)
        earlier = jnp.where(lax.broadcasted_iota(jnp.int32, (tm, tm), 0)
                            < lax.broadcasted_iota(jnp.int32, (tm, tm), 1), 1.0, 0.0).astype(BF16)
        rank = _dot(jnp.where(routed, 1.0, 0.0).astype(BF16), earlier)
        rank_sc[...] = jnp.where(routed, rank, -1.0)
        acc_sc[...] = jnp.zeros_like(acc_sc)

    rank_e = rank_sc[pl.ds(e, 1), :]
    count = jnp.sum(jnp.where(rank_e >= 0.0, 1.0, 0.0)).astype(jnp.int32)
    n_extra = lax.shift_right_logical(jnp.maximum(count - c0, 0) + (MOE_CHUNK - 1), MOE_CHUNK.bit_length() - 1)

    def for_chunks(fn):
        fn(0, c0)

        def body(k, carry):
            fn(pl.multiple_of(c0 + k * MOE_CHUNK, MOE_CHUNK), MOE_CHUNK)
            return carry

        lax.fori_loop(0, n_extra, body, 0)

    def selection(base, rows):
        slot = (base + lax.broadcasted_iota(jnp.int32, (rows, 1), 0)).astype(F32)
        return jnp.where(rank_e == slot, 1.0, 0.0).astype(BF16)

    def gather(base, rows):
        sel = selection(base, rows)
        xe_sc[pl.ds(base, rows), :] = _dot(sel, xn_sc[...]).astype(BF16)
        gh, gm, gl = _split3(gate_sc[...])
        gates = (_dot_nt(sel, gh) + _dot_nt(sel, gm)) + _dot_nt(sel, gl)
        lane = lax.broadcasted_iota(jnp.int32, (rows, e_rows), 1)
        ge_sc[pl.ds(base, rows), :] = jnp.sum(jnp.where(lane == e, gates, 0.0), axis=1, keepdims=True)
        ye_sc[pl.ds(base, rows), :] = jnp.zeros((rows, ye_sc.shape[1]), F32)

    def expert(base, rows):
        xe = xe_sc[pl.ds(base, rows), :]
        gate = _dot(xe, wg_ref[...])
        up = _dot(xe, wu_ref[...])
        act = (gate * jax.nn.sigmoid(gate) * up * ge_sc[pl.ds(base, rows), :]).astype(BF16)
        ye_sc[pl.ds(base, rows), :] += _dot(act, wd_ref[...])

    def scatter(base, rows):
        sel = selection(base, rows)
        y = ye_sc[pl.ds(base, rows), :]
        yh = y.astype(BF16)
        yl = (y - yh.astype(F32)).astype(BF16)
        acc_sc[...] += (lax.dot_general(sel, yh, _TN, preferred_element_type=F32)
                        + lax.dot_general(sel, yl, _TN, preferred_element_type=F32))

    @pl.when(j == 0)
    def _():
        for_chunks(gather)

    for_chunks(expert)

    @pl.when(j == pl.num_programs(2) - 1)
    def _():
        for_chunks(scatter)

    @pl.when((e == pl.num_programs(1) - 1) & (j == pl.num_programs(2) - 1))
    def _():
        o_ref[...] = _rmsnorm(x_ref[...] + acc_sc[...], gf_ref[...])


def _moe(x, gain, w_router_t3, w_gate_up, w_down, gain_final, tm, tf):
    n, d = x.shape
    n_e, d_ff, _ = w_down.shape
    n_f = d_ff // tf
    assert tm % MOE_CHUNK == 0
    c0 = max(MOE_CHUNK, tm * 2 // n_e)
    row_spec = pl.BlockSpec((tm, d), lambda i, e, j: (i, 0))
    vec_spec = pl.BlockSpec((1, d), lambda i, e, j: (0, 0))
    return pl.pallas_call(
        functools.partial(_moe_kernel, c0=c0),
        grid=(n // tm, n_e, n_f),
        in_specs=[row_spec, vec_spec,
                  pl.BlockSpec((N_SPLIT, LANES, d), lambda i, e, j: (0, 0, 0)),
                  pl.BlockSpec((None, d, tf), lambda i, e, j: (e, 0, j)),
                  pl.BlockSpec((None, d, tf), lambda i, e, j: (e, 0, j + n_f)),
                  pl.BlockSpec((None, tf, d), lambda i, e, j: (e, j, 0)),
                  vec_spec],
        out_specs=row_spec,
        out_shape=jax.ShapeDtypeStruct((n, d), F32),
        scratch_shapes=[pltpu.VMEM((tm, d), BF16), pltpu.VMEM((LANES, tm), F32), pltpu.VMEM((LANES, tm), F32),
                        pltpu.VMEM((tm, d), BF16), pltpu.VMEM((tm, 1), F32), pltpu.VMEM((tm, d), F32),
                        pltpu.VMEM((tm, d), F32)],
        compiler_params=_params("parallel", "arbitrary", "arbitrary"),
        name="moe_swiglu_final_norm",
    )(x, gain, w_router_t3, w_gate_up, w_gate_up, w_down, gain_final)


def _row_tile(n, target):
    t = min(n, target)
    while n % t:
        t //= 2
    return t


def kernel(x_prompt, x_sample, cache_k, cache_v, cache_logf, state_pool, page_table, l0_norm_attn, l0_w_qkvf, l0_b_f, l0_w_o, l0_norm_ffn, l0_w_gate_up, l0_w_down, l1_norm_pool, l1_w_group, l1_pool_scale, l1_norm_ffn, l1_w_router, l1_w_gate_up, l1_w_down, final_norm):
    b_p, s_p, d = x_prompt.shape
    b_s, t_new, _ = x_sample.shape
    assert b_p == 1 and d == N_HEADS * HEAD_DIM
    n_phys, page = cache_k.shape[:2]
    past_len = page_table.shape[1] * page
    d_ff = l0_w_down.shape[0]

    row = lambda v: v.reshape(1, -1).astype(F32)
    w_t = l0_w_qkvf.T.astype(BF16)
    wf_pad = jnp.pad(w_t[3 * d:], ((0, LANES - N_HEADS), (0, 0)))
    bf = jnp.pad(l0_b_f, (0, LANES - N_HEADS)).reshape(1, LANES)
    bft = l0_b_f.reshape(N_HEADS, 1)
    wo = l0_w_o.astype(BF16)
    w0_gu = l0_w_gate_up.astype(BF16)
    w0_d = l0_w_down.astype(BF16)
    wgrp = l1_w_group.astype(BF16)
    wr_pad = jnp.pad(l1_w_router.T, ((0, LANES - N_EXPERTS), (0, 0)))
    wr_hi = wr_pad.astype(BF16)
    wr_r1 = wr_pad - wr_hi.astype(F32)
    wr_mid = wr_r1.astype(BF16)
    wr_lo = (wr_r1 - wr_mid.astype(F32)).astype(BF16)
    wr3 = jnp.stack([wr_hi, wr_mid, wr_lo])
    w1_gu = l1_w_gate_up.astype(BF16)
    w1_d = l1_w_down.astype(BF16)
    tf = 512 if d_ff % 512 == 0 else d_ff

    xp = x_prompt.reshape(s_p, d)
    xs = x_sample.reshape(b_s * t_new, d)
    n_s = b_s * t_new

    blk = _row_tile(s_p, 512)
    qt_p, kh_p, kt_p, vt_p, vtb_p, lf_p, lft_p = _qkvf_prompt(xp, row(l0_norm_attn), w_t, wf_pad, bf, bft, blk)
    q_s, k_s, v_s, lf_s = _qkvf_sample(xs, row(l0_norm_attn), w_t, wf_pad, bf, _row_tile(n_s, 512))

    attn_p = _prompt_attn(qt_p, _key_aug(lf_p, kh_p, blk), vtb_p, blk)

    feat_major = lambda c: jnp.transpose(c, (0, 2, 3, 1)).reshape(n_phys, d, page)
    n_pp = 8 if page_table.shape[1] % 8 == 0 else 1
    attn_s = _sample_attn(page_table, feat_major(cache_k), feat_major(cache_v), jnp.swapaxes(cache_logf, 1, 2),
                          q_s, k_s, v_s, lf_s, n_pp)

    hp = _oproj(attn_p, xp, wo, blk, True)
    hs = _oproj(attn_s, xs, wo, _row_tile(n_s, 512), False)

    hp = _ffn(hp, row(l0_norm_ffn), w0_gu, w0_d, _row_tile(s_p, 1024), tf)
    hs = _ffn(hs, row(l0_norm_ffn), w0_gu, w0_d, _row_tile(n_s, 1024), tf)

    hp, tail_p = _pool_prompt(hp, row(l1_norm_pool), wgrp, row(l1_pool_scale), _row_tile(s_p, 512))
    state_pad = jnp.pad(state_pool, ((0, 0), (POOL_PAD - POOL_HIST, 0), (0, 0)))
    hs, hist_s = _pool_sample(hs, state_pad, row(l1_norm_pool), wgrp, row(l1_pool_scale), t_new, past_len)

    yp = _moe(hp, row(l1_norm_ffn), wr3, w1_gu, w1_d, row(final_norm), _row_tile(s_p, 1024), tf)
    ys = _moe(hs, row(l1_norm_ffn), wr3, w1_gu, w1_d, row(final_norm), _row_tile(n_s, 1024), tf)

    heads_t = lambda a: jnp.transpose(a.reshape(1, N_HEADS, HEAD_DIM, s_p), (0, 3, 1, 2))
    heads = lambda a: a.reshape(b_s, t_new, N_HEADS, HEAD_DIM)
    return (yp.reshape(b_p, s_p, d), ys.reshape(b_s, t_new, d),
            heads_t(kt_p), heads_t(vt_p), jnp.transpose(lft_p.reshape(1, N_HEADS, s_p), (0, 2, 1)),
            tail_p[POOL_PAD - POOL_HIST:].reshape(b_p, POOL_HIST, d),
            heads(k_s), heads(v_s), lf_s[:, :N_HEADS].reshape(b_s, t_new, N_HEADS),
            hist_s[:, POOL_PAD - POOL_HIST:, :])
```

```python
import functools

import jax
import jax.numpy as jnp
from jax import lax
from jax.experimental import pallas as pl
from jax.experimental.pallas import tpu as pltpu

N_HEADS = 16
HEAD_DIM = 64
RMS_EPS = 1e-6
POOL_WINDOWS = (2, 4, 8, 16)
POOL_HIST = max(POOL_WINDOWS) - 1
POOL_PAD = POOL_HIST + 1
N_EXPERTS = 8
LANES = 128
N_SPLIT = 3
NEG_BIG = -1e30
LOG2_E = 1.4426950408889634
MOE_CHUNK = 128
VMEM_LIMIT_BYTES = 56 * 1024 * 1024

F32 = jnp.float32
BF16 = jnp.bfloat16
_NT = (((1,), (1,)), ((), ()))
_TN = (((0,), (0,)), ((), ()))


def _params(*sem):
    return pltpu.CompilerParams(dimension_semantics=sem, vmem_limit_bytes=VMEM_LIMIT_BYTES)


def _rmsnorm(x, gain):
    ms = jnp.mean(x * x, axis=-1, keepdims=True)
    return x * lax.rsqrt(ms + RMS_EPS) * gain


def _split3(x):
    hi = x.astype(BF16)
    r1 = x - hi.astype(F32)
    mid = r1.astype(BF16)
    lo = (r1 - mid.astype(F32)).astype(BF16)
    return hi, mid, lo


def _dot(a, b):
    return jnp.dot(a, b, preferred_element_type=F32)


def _dot_nt(a, b):
    return lax.dot_general(a, b, _NT, preferred_element_type=F32)


def _dot_exact_rhs(a_bf16, x_f32):
    hi, mid, lo = _split3(x_f32)
    return (_dot(a_bf16, hi) + _dot(a_bf16, mid)) + _dot(a_bf16, lo)


def _dot_exact_lhs(x_f32, a_bf16):
    hi, mid, lo = _split3(x_f32)
    return (_dot(hi, a_bf16) + _dot(mid, a_bf16)) + _dot(lo, a_bf16)


def _log_sigmoid(z):
    return jnp.minimum(z, 0.0) - jnp.log1p(jnp.exp(-jnp.abs(z)))


def _full_spec(shape):
    return pl.BlockSpec(shape, lambda *_: (0,) * len(shape))


def _qkvf_prompt_kernel(x_ref, g_ref, wq_ref, wk_ref, wv_ref, wf_ref, wfp_ref, bf_ref, bft_ref,
                        qt_ref, kh_ref, kt_ref, vt_ref, vtb_ref, lf_ref, lft_ref):
    xn = _rmsnorm(x_ref[...], g_ref[...]).astype(BF16)
    qt_ref[...] = (_dot_nt(wq_ref[...], xn) * (LOG2_E * HEAD_DIM ** -0.5)).astype(BF16)
    kt_ref[...] = _dot_nt(wk_ref[...], xn)
    vt = _dot_nt(wv_ref[...], xn)
    vt_ref[...] = vt
    vtb_ref[...] = vt.astype(BF16)
    k = _dot_nt(xn, wk_ref[...])
    for h in range(N_HEADS):
        kh_ref[h] = k[:, h * HEAD_DIM:(h + 1) * HEAD_DIM].astype(BF16)
    lf_ref[...] = _log_sigmoid(_dot_nt(xn, wfp_ref[...]) + bf_ref[...])
    lft_ref[...] = _log_sigmoid(_dot_nt(wf_ref[...], xn) + bft_ref[...])


def _qkvf_prompt(x, gain, w_t, wf_pad, bf, bft, tm):
    n, d = x.shape
    feat_spec = pl.BlockSpec((d, tm), lambda i: (0, i))
    w_spec = lambda blk: pl.BlockSpec((d, d), lambda i: (blk, 0))
    feat = lambda dt: jax.ShapeDtypeStruct((d, n), dt)
    return pl.pallas_call(
        _qkvf_prompt_kernel,
        grid=(n // tm,),
        in_specs=[pl.BlockSpec((tm, d), lambda i: (i, 0)), _full_spec((1, d)),
                  w_spec(0), w_spec(1), w_spec(2),
                  pl.BlockSpec((N_HEADS, d), lambda i: (3 * d // N_HEADS, 0)),
                  _full_spec((LANES, d)), _full_spec((1, LANES)), _full_spec((N_HEADS, 1))],
        out_specs=[feat_spec, pl.BlockSpec((N_HEADS, tm, HEAD_DIM), lambda i: (0, i, 0)),
                   feat_spec, feat_spec, feat_spec,
                   pl.BlockSpec((tm, LANES), lambda i: (i, 0)),
                   pl.BlockSpec((N_HEADS, tm), lambda i: (0, i))],
        out_shape=[feat(BF16), jax.ShapeDtypeStruct((N_HEADS, n, HEAD_DIM), BF16),
                   feat(F32), feat(F32), feat(BF16),
                   jax.ShapeDtypeStruct((n, LANES), F32), jax.ShapeDtypeStruct((N_HEADS, n), F32)],
        compiler_params=_params("parallel"),
        name="qkvf_proj_prompt",
    )(x, gain, w_t, w_t, w_t, w_t, wf_pad, bf, bft)


def _qkvf_sample_kernel(x_ref, g_ref, wq_ref, wk_ref, wv_ref, wfp_ref, bf_ref, q_ref, k_ref, v_ref, lf_ref):
    xn = _rmsnorm(x_ref[...], g_ref[...]).astype(BF16)
    q_ref[...] = _dot_nt(xn, wq_ref[...]) * (HEAD_DIM ** -0.5)
    k_ref[...] = _dot_nt(xn, wk_ref[...])
    v_ref[...] = _dot_nt(xn, wv_ref[...])
    lf_ref[...] = _log_sigmoid(_dot_nt(xn, wfp_ref[...]) + bf_ref[...])


def _qkvf_sample(x, gain, w_t, wf_pad, bf, tm):
    n, d = x.shape
    row_spec = pl.BlockSpec((tm, d), lambda i: (i, 0))
    w_spec = lambda blk: pl.BlockSpec((d, d), lambda i: (blk, 0))
    rows = jax.ShapeDtypeStruct((n, d), F32)
    return pl.pallas_call(
        _qkvf_sample_kernel,
        grid=(n // tm,),
        in_specs=[row_spec, _full_spec((1, d)), w_spec(0), w_spec(1), w_spec(2),
                  _full_spec((LANES, d)), _full_spec((1, LANES))],
        out_specs=[row_spec, row_spec, row_spec, pl.BlockSpec((tm, LANES), lambda i: (i, 0))],
        out_shape=[rows, rows, rows, jax.ShapeDtypeStruct((n, LANES), F32)],
        compiler_params=_params("parallel"),
        name="qkvf_proj_sample",
    )(x, gain, w_t, w_t, w_t, wf_pad, bf)


def _key_aug_kernel(lf_ref, kh_ref, ka_ref, c_ref):
    @pl.when(pl.program_id(0) == 0)
    def _():
        c_ref[...] = jnp.zeros_like(c_ref)

    t = lf_ref.shape[0]
    row = lax.broadcasted_iota(jnp.int32, (t, t), 0)
    col = lax.broadcasted_iota(jnp.int32, (t, t), 1)
    fc = _dot_exact_rhs((col <= row).astype(BF16), lf_ref[...]) + c_ref[...]
    c_ref[...] = fc[t - 1:t, :]
    lane = lax.broadcasted_iota(jnp.int32, (t, HEAD_DIM), 1)
    for h in range(N_HEADS):
        terms = _split3(jnp.broadcast_to(-LOG2_E * fc[:, h:h + 1], (t, HEAD_DIM)))
        aug = jnp.zeros((t, HEAD_DIM), F32)
        for i in range(N_SPLIT):
            aug = jnp.where(lane == i, terms[i].astype(F32), aug)
        ka_ref[h] = jnp.concatenate([kh_ref[h], aug.astype(BF16)], axis=1)


def _key_aug(lf, kh, t):
    s = lf.shape[0]
    return pl.pallas_call(
        _key_aug_kernel,
        grid=(s // t,),
        in_specs=[pl.BlockSpec((t, LANES), lambda i: (i, 0)),
                  pl.BlockSpec((N_HEADS, t, HEAD_DIM), lambda i: (0, i, 0))],
        out_specs=pl.BlockSpec((N_HEADS, t, 2 * HEAD_DIM), lambda i: (0, i, 0)),
        out_shape=jax.ShapeDtypeStruct((N_HEADS, s, 2 * HEAD_DIM), BF16),
        scratch_shapes=[pltpu.VMEM((1, LANES), F32)],
        compiler_params=_params("arbitrary"),
        name="key_aug_cumsum",
    )(lf, kh)


def _prompt_attn_kernel(qt_ref, ka_ref, vt_ref, o_ref, s_sc, mc_sc, m_sc, l_sc, acc_sc, *, blk):
    qi = pl.program_id(1)
    ones_rows = (lax.broadcasted_iota(jnp.int32, (HEAD_DIM, blk), 0) < N_SPLIT).astype(BF16)
    qa = jnp.concatenate([qt_ref[...], ones_rows], axis=0)
    m_sc[...] = jnp.full_like(m_sc, NEG_BIG)
    l_sc[...] = jnp.zeros_like(l_sc)
    acc_sc[...] = jnp.zeros_like(acc_sc)
    kidx = lax.broadcasted_iota(jnp.int32, (blk, blk), 0)
    qidx = lax.broadcasted_iota(jnp.int32, (blk, blk), 1)

    def scores(ki, slot, causal):
        ks = pl.multiple_of(ki * blk, blk)
        s = _dot(ka_ref[0, pl.ds(ks, blk), :], qa)
        if causal:
            s = jnp.where(kidx <= qidx, s, NEG_BIG)
        s_sc[slot] = s
        mc_sc[slot] = jnp.max(s, axis=0, keepdims=True)

    def softmax_pv(ki, slot):
        ks = pl.multiple_of(ki * blk, blk)
        m_prev = m_sc[...]
        m_new = jnp.maximum(m_prev, mc_sc[slot])
        alpha = jnp.exp2(m_prev - m_new)
        p = jnp.exp2(s_sc[slot] - m_new)
        l_sc[...] = alpha * l_sc[...] + jnp.sum(p, axis=0, keepdims=True)
        acc_sc[...] = alpha * acc_sc[...] + _dot(vt_ref[:, pl.ds(ks, blk)], p.astype(BF16))
        m_sc[...] = m_new

    @pl.when(qi == 0)
    def _():
        scores(0, 0, True)
        softmax_pv(0, 0)

    @pl.when(qi > 0)
    def _():
        scores(0, 0, False)
        n_pairs = (qi - 1) // 2

        def body(k, carry):
            scores(2 * k + 1, 1, False)
            softmax_pv(2 * k, 0)
            scores(2 * k + 2, 0, False)
            softmax_pv(2 * k + 1, 1)
            return carry

        lax.fori_loop(0, n_pairs, body, 0)
        i0 = 2 * n_pairs

        @pl.when(qi - i0 == 2)
        def _():
            scores(i0 + 1, 1, False)
            softmax_pv(i0, 0)
            scores(qi, 0, True)
            softmax_pv(i0 + 1, 1)
            softmax_pv(qi, 0)

        @pl.when(qi - i0 == 1)
        def _():
            scores(qi, 1, True)
            softmax_pv(i0, 0)
            softmax_pv(qi, 1)

    o_ref[...] = (acc_sc[...] / l_sc[...]).astype(o_ref.dtype)


def _prompt_attn(qt, ka, vt, blk):
    d, s = qt.shape
    return pl.pallas_call(
        functools.partial(_prompt_attn_kernel, blk=blk),
        grid=(N_HEADS, s // blk),
        in_specs=[pl.BlockSpec((HEAD_DIM, blk), lambda h, i: (h, i)),
                  pl.BlockSpec((1, s, 2 * HEAD_DIM), lambda h, i: (h, 0, 0)),
                  pl.BlockSpec((HEAD_DIM, s), lambda h, i: (h, 0))],
        out_specs=pl.BlockSpec((HEAD_DIM, blk), lambda h, i: (h, i)),
        out_shape=jax.ShapeDtypeStruct((d, s), BF16),
        scratch_shapes=[pltpu.VMEM((2, blk, blk), F32), pltpu.VMEM((2, 1, blk), F32),
                        pltpu.VMEM((1, blk), F32), pltpu.VMEM((1, blk), F32),
                        pltpu.VMEM((HEAD_DIM, blk), F32)],
        compiler_params=_params("parallel", "parallel"),
        name="fox_prompt_attn",
    )(qt, ka, vt)


def _sample_attn_kernel(pt_ref, *refs, n_pp, page, t_new):
    del pt_ref
    k_refs = refs[:n_pp]
    v_refs = refs[n_pp:2 * n_pp]
    lf_refs = refs[2 * n_pp:3 * n_pp]
    q_ref, kn_ref, vn_ref, lfn_ref, o_ref, qbd_sc, m_sc, l_sc, acc_sc, suf_sc = refs[3 * n_pp:]
    j = pl.program_id(1)
    rows = N_HEADS * t_new
    d = N_HEADS * HEAD_DIM
    own = (lax.broadcasted_iota(jnp.int32, (rows, d), 1) // HEAD_DIM
           == lax.broadcasted_iota(jnp.int32, (rows, d), 0) // t_new)

    def online_update(s, pv_fn):
        m_prev = m_sc[...]
        m_new = jnp.maximum(m_prev, jnp.max(s, axis=1, keepdims=True))
        alpha = jnp.exp(m_prev - m_new)
        p = jnp.exp(s - m_new)
        l_sc[...] = alpha * l_sc[...] + jnp.sum(p, axis=1, keepdims=True)
        acc_sc[...] = alpha * acc_sc[...] + pv_fn(p.astype(BF16))
        m_sc[...] = m_new

    def add_head_rows(s, bias):
        return jnp.concatenate(
            [s[h * t_new:(h + 1) * t_new, :] + bias[h:h + 1, :] for h in range(N_HEADS)], axis=0)

    @pl.when(j == 0)
    def _():
        m_sc[...] = jnp.full_like(m_sc, NEG_BIG)
        l_sc[...] = jnp.zeros_like(l_sc)
        acc_sc[...] = jnp.zeros_like(acc_sc)
        suf_sc[...] = jnp.zeros_like(suf_sc)
        q_rep = jnp.concatenate([q_ref[...]] * N_HEADS, axis=0)
        qbd_sc[...] = jnp.where(own, q_rep, 0.0).astype(BF16)
        r8 = lax.broadcasted_iota(jnp.int32, (t_new, t_new), 0)
        c8 = lax.broadcasted_iota(jnp.int32, (t_new, t_new), 1)
        cum = _dot_exact_rhs((c8 <= r8).astype(BF16), lfn_ref[...])
        cum_pad = jnp.concatenate([cum, jnp.zeros((LANES - t_new, LANES), F32)], axis=0)
        eye = (lax.broadcasted_iota(jnp.int32, (N_HEADS, LANES), 0)
               == lax.broadcasted_iota(jnp.int32, (N_HEADS, LANES), 1)).astype(BF16)
        hi, mid, lo = _split3(cum_pad)
        cum_t = (_dot_nt(eye, hi) + _dot_nt(eye, mid)) + _dot_nt(eye, lo)
        zpad = jnp.zeros((LANES - t_new, d), F32)
        kn = jnp.concatenate([kn_ref[...], zpad], axis=0).astype(BF16)
        vn = jnp.concatenate([vn_ref[...], zpad], axis=0).astype(BF16)
        s = add_head_rows(_dot_nt(qbd_sc[...], kn), -cum_t)
        qt = lax.broadcasted_iota(jnp.int32, (rows, LANES), 0) % t_new
        kt = lax.broadcasted_iota(jnp.int32, (rows, LANES), 1)
        s = jnp.where(kt <= qt, s, NEG_BIG)
        online_update(s, lambda p: _dot(p, vn))

    rp = lax.broadcasted_iota(jnp.int32, (page, page), 0)
    cp = lax.broadcasted_iota(jnp.int32, (page, page), 1)
    newer = (rp > cp).astype(BF16)
    carry = suf_sc[...]
    suf_pages = [None] * n_pp
    for i in reversed(range(n_pp)):
        lf = lf_refs[i][...]
        suf_pages[i] = _dot_exact_lhs(lf, newer) + carry
        carry = carry + jnp.sum(lf, axis=1, keepdims=True)
    suf_sc[...] = carry
    suf = jnp.concatenate(suf_pages, axis=1)

    kt_pages = jnp.concatenate([r[...] for r in k_refs], axis=1).astype(BF16)
    vt_pages = jnp.concatenate([r[...] for r in v_refs], axis=1).astype(BF16)
    s = add_head_rows(_dot(qbd_sc[...], kt_pages), suf)
    online_update(s, lambda p: _dot_nt(p, vt_pages))

    @pl.when(j == pl.num_programs(1) - 1)
    def _():
        out = jnp.where(own, acc_sc[...] / l_sc[...], 0.0)
        o = out[0:t_new, :]
        for h in range(1, N_HEADS):
            o = o + out[h * t_new:(h + 1) * t_new, :]
        o_ref[...] = o


def _sample_attn(page_table, cache_kt, cache_vt, cache_lft, q_s, k_s, v_s, lf_s, n_pp):
    n_seq, n_pages = page_table.shape
    _, d, page = cache_kt.shape
    t_new = q_s.shape[0] // n_seq
    n_grp = n_pages // n_pp

    def page_idx(i):
        return lambda b, j, pt: (pt[b, (n_grp - 1 - j) * n_pp + i], 0, 0)

    kv_specs = [pl.BlockSpec((None, d, page), page_idx(i)) for i in range(n_pp)]
    lf_specs = [pl.BlockSpec((None, N_HEADS, page), page_idx(i)) for i in range(n_pp)]
    tok_spec = pl.BlockSpec((t_new, d), lambda b, j, pt: (b, 0))
    rows = N_HEADS * t_new
    grid_spec = pltpu.PrefetchScalarGridSpec(
        num_scalar_prefetch=1,
        grid=(n_seq, n_grp),
        in_specs=kv_specs + kv_specs + lf_specs + [
            tok_spec, tok_spec, tok_spec, pl.BlockSpec((t_new, LANES), lambda b, j, pt: (b, 0))],
        out_specs=tok_spec,
        scratch_shapes=[pltpu.VMEM((rows, d), BF16), pltpu.VMEM((rows, 1), F32), pltpu.VMEM((rows, 1), F32),
                        pltpu.VMEM((rows, d), F32), pltpu.VMEM((N_HEADS, 1), F32)],
    )
    return pl.pallas_call(
        functools.partial(_sample_attn_kernel, n_pp=n_pp, page=page, t_new=t_new),
        grid_spec=grid_spec,
        out_shape=jax.ShapeDtypeStruct((n_seq * t_new, d), F32),
        compiler_params=_params("parallel", "arbitrary"),
        name="fox_sample_attn",
    )(page_table, *([cache_kt] * n_pp), *([cache_vt] * n_pp), *([cache_lft] * n_pp), q_s, k_s, v_s, lf_s)


def _oproj_kernel(a_ref, x_ref, wo_ref, o_ref, *, feature_major):
    a = a_ref[...].astype(BF16)
    if feature_major:
        proj = lax.dot_general(a, wo_ref[...], _TN, preferred_element_type=F32)
    else:
        proj = _dot(a, wo_ref[...])
    o_ref[...] = x_ref[...] + proj


def _oproj(attn, x, wo, tm, feature_major):
    n, d = x.shape
    row_spec = pl.BlockSpec((tm, d), lambda i: (i, 0))
    attn_spec = pl.BlockSpec((d, tm), lambda i: (0, i)) if feature_major else row_spec
    return pl.pallas_call(
        functools.partial(_oproj_kernel, feature_major=feature_major),
        grid=(n // tm,),
        in_specs=[attn_spec, row_spec, _full_spec((d, d))],
        out_specs=row_spec,
        out_shape=jax.ShapeDtypeStruct((n, d), F32),
        compiler_params=_params("parallel"),
        name="attn_out_proj",
    )(attn, x, wo)


def _tile_gate_up(w, tf):
    *lead, d, two_d_ff = w.shape
    w = w.astype(BF16).reshape(*lead, d, two_d_ff // tf, tf)
    return jnp.moveaxis(w, -2, -3)


def _swiglu_act(x_bf16, wg_ref, wu_ref):
    gate = _dot(x_bf16, wg_ref[...])
    return gate * jax.nn.sigmoid(gate) * _dot(x_bf16, wu_ref[...])


def _ffn_kernel(x_ref, g_ref, wg_ref, wu_ref, wd_ref, o_ref, xn_sc, acc_sc):
    j = pl.program_id(1)

    @pl.when(j == 0)
    def _():
        xn_sc[...] = _rmsnorm(x_ref[...], g_ref[...]).astype(BF16)
        acc_sc[...] = jnp.zeros_like(acc_sc)

    act = _swiglu_act(xn_sc[...], wg_ref, wu_ref).astype(BF16)
    acc_sc[...] += _dot(act, wd_ref[...])

    @pl.when(j == pl.num_programs(1) - 1)
    def _():
        o_ref[...] = x_ref[...] + acc_sc[...]


def _ffn(x, gain, w_gate_up_tiled, w_down, tm):
    n, d = x.shape
    two_n_f, _, tf = w_gate_up_tiled.shape
    n_f = two_n_f // 2
    row_spec = pl.BlockSpec((tm, d), lambda i, j: (i, 0))
    return pl.pallas_call(
        _ffn_kernel,
        grid=(n // tm, n_f),
        in_specs=[row_spec, pl.BlockSpec((1, d), lambda i, j: (0, 0)),
                  pl.BlockSpec((None, d, tf), lambda i, j: (j, 0, 0)),
                  pl.BlockSpec((None, d, tf), lambda i, j: (j + n_f, 0, 0)),
                  pl.BlockSpec((tf, d), lambda i, j: (j, 0))],
        out_specs=row_spec,
        out_shape=jax.ShapeDtypeStruct((n, d), F32),
        scratch_shapes=[pltpu.VMEM((tm, d), BF16), pltpu.VMEM((tm, d), F32)],
        compiler_params=_params("parallel", "arbitrary"),
        name="dense_swiglu",
    )(x, gain, w_gate_up_tiled, w_gate_up_tiled, w_down)


def _pool_mix(ext_ref, x, u, pos0, wgrp_ref, scale_ref, o_ref, n_rows):
    gdim = u.shape[1] // len(POOL_WINDOWS)
    pos = pos0 + lax.broadcasted_iota(jnp.int32, (n_rows, 1), 0)
    for g, w in enumerate(POOL_WINDOWS):
        cols = pl.ds(g * gdim, gdim)
        win = ext_ref[pl.ds(POOL_PAD, n_rows), cols]
        for i in range(1, w):
            win = win + ext_ref[pl.ds(POOL_PAD - i, n_rows), cols]
        count = jnp.minimum(pos + 1, w).astype(F32)
        pooled = (win / count - u[:, g * gdim:(g + 1) * gdim]).astype(BF16)
        mixed = _dot(pooled, wgrp_ref[g]) * scale_ref[:, cols]
        o_ref[:, cols] = x[:, g * gdim:(g + 1) * gdim] + mixed


def _pool_prompt_kernel(x_ref, g_ref, wgrp_ref, scale_ref, o_ref, tail_ref, ext_sc, *, tm):
    i = pl.program_id(0)

    @pl.when(i == 0)
    def _():
        ext_sc[pl.ds(0, POOL_PAD), :] = jnp.zeros((POOL_PAD, ext_sc.shape[1]), F32)

    @pl.when(i > 0)
    def _():
        ext_sc[pl.ds(0, POOL_PAD), :] = ext_sc[pl.ds(tm, POOL_PAD), :]

    x = x_ref[...]
    u = _rmsnorm(x, g_ref[...])
    ext_sc[pl.ds(POOL_PAD, tm), :] = u
    tail_ref[...] = u[tm - POOL_PAD:, :]
    _pool_mix(ext_sc, x, u, i * tm, wgrp_ref, scale_ref, o_ref, tm)


def _pool_prompt(x, gain, w_group, scale, tm):
    n, d = x.shape
    gdim = d // len(POOL_WINDOWS)
    row_spec = pl.BlockSpec((tm, d), lambda i: (i, 0))
    return pl.pallas_call(
        functools.partial(_pool_prompt_kernel, tm=tm),
        grid=(n // tm,),
        in_specs=[row_spec, pl.BlockSpec((1, d), lambda i: (0, 0)),
                  pl.BlockSpec((len(POOL_WINDOWS), gdim, gdim), lambda i: (0, 0, 0)),
                  pl.BlockSpec((1, d), lambda i: (0, 0))],
        out_specs=[row_spec, pl.BlockSpec((POOL_PAD, d), lambda i: (0, 0))],
        out_shape=[jax.ShapeDtypeStruct((n, d), F32), jax.ShapeDtypeStruct((POOL_PAD, d), F32)],
        scratch_shapes=[pltpu.VMEM((POOL_PAD + tm, d), F32)],
        compiler_params=_params("arbitrary"),
        name="pool_mix_prompt",
    )(x, gain, w_group, scale)


def _pool_sample_kernel(x_ref, st_ref, g_ref, wgrp_ref, scale_ref, o_ref, hist_ref, ext_sc, *, t_new, past_len):
    x = x_ref[...]
    u = _rmsnorm(x, g_ref[...])
    ext_sc[pl.ds(0, POOL_PAD), :] = st_ref[...]
    ext_sc[pl.ds(POOL_PAD, t_new), :] = u
    hist_ref[...] = ext_sc[pl.ds(t_new, POOL_PAD), :]
    _pool_mix(ext_sc, x, u, past_len, wgrp_ref, scale_ref, o_ref, t_new)


def _pool_sample(x, state_pad, gain, w_group, scale, t_new, past_len):
    n, d = x.shape
    gdim = d // len(POOL_WINDOWS)
    row_spec = pl.BlockSpec((t_new, d), lambda b: (b, 0))
    st_spec = pl.BlockSpec((None, POOL_PAD, d), lambda b: (b, 0, 0))
    return pl.pallas_call(
        functools.partial(_pool_sample_kernel, t_new=t_new, past_len=past_len),
        grid=(n // t_new,),
        in_specs=[row_spec, st_spec, pl.BlockSpec((1, d), lambda b: (0, 0)),
                  pl.BlockSpec((len(POOL_WINDOWS), gdim, gdim), lambda b: (0, 0, 0)),
                  pl.BlockSpec((1, d), lambda b: (0, 0))],
        out_specs=[row_spec, st_spec],
        out_shape=[jax.ShapeDtypeStruct((n, d), F32), jax.ShapeDtypeStruct(state_pad.shape, F32)],
        scratch_shapes=[pltpu.VMEM((POOL_PAD + t_new, d), F32)],
        compiler_params=_params("parallel"),
        name="pool_mix_sample",
    )(x, state_pad, gain, w_group, scale)


def _moe_kernel(x_ref, g_ref, wrt_ref, wg_ref, wu_ref, wd_ref, gf_ref, o_ref,
                xn_sc, gate_sc, rank_sc, xe_sc, ge_sc, ye_sc, acc_sc, *, c0):
    e = pl.program_id(1)
    j = pl.program_id(2)
    tm = x_ref.shape[0]
    e_rows = gate_sc.shape[0]

    @pl.when((e == 0) & (j == 0))
    def _():
        xn = _rmsnorm(x_ref[...], g_ref[...])
        xh, xm, xl = _split3(xn)
        xn_sc[...] = xh
        wh, wm, wl = wrt_ref[0], wrt_ref[1], wrt_ref[2]
        logits = ((_dot_nt(wl, xh) + _dot_nt(wm, xm) + _dot_nt(wh, xl))
                  + (_dot_nt(wm, xh) + _dot_nt(wh, xm))) + _dot_nt(wh, xh)
        erow = lax.broadcasted_iota(jnp.int32, (e_rows, tm), 0)
        logits = jnp.where(erow < N_EXPERTS, logits, -jnp.inf)
        top1 = jnp.max(logits, axis=0, keepdims=True)
        idx1 = jnp.min(jnp.where(logits == top1, erow, e_rows), axis=0, keepdims=True)
        rest = jnp.where(erow == idx1, -jnp.inf, logits)
        top2 = jnp.max(rest, axis=0, keepdims=True)
        idx2 = jnp.min(jnp.where(rest == top2, erow, e_rows), axis=0, keepdims=True)
        e2 = jnp.exp(top2 - top1)
        denom = 1.0 + e2
        gate_sc[...] = jnp.where(erow == idx1, 1.0 / denom, 0.0) + jnp.where(erow == idx2, e2 / denom, 0.0)
        routed = (erow == idx1) | (erow == idx2)
        earlier = jnp.where(lax.broadcasted_iota(jnp.int32, (tm, tm), 0)
                            < lax.broadcasted_iota(jnp.int32, (tm, tm), 1), 1.0, 0.0).astype(BF16)
        rank = _dot(jnp.where(routed, 1.0, 0.0).astype(BF16), earlier)
        rank_sc[...] = jnp.where(routed, rank, -1.0)
        acc_sc[...] = jnp.zeros_like(acc_sc)

    rank_e = rank_sc[pl.ds(e, 1), :]
    count = jnp.sum(jnp.where(rank_e >= 0.0, 1.0, 0.0)).astype(jnp.int32)
    n_extra = lax.shift_right_logical(jnp.maximum(count - c0, 0) + (MOE_CHUNK - 1), MOE_CHUNK.bit_length() - 1)

    def for_chunks(fn):
        fn(0, c0)

        def body(k, carry):
            fn(pl.multiple_of(c0 + k * MOE_CHUNK, MOE_CHUNK), MOE_CHUNK)
            return carry

        lax.fori_loop(0, n_extra, body, 0)

    def selection(base, rows):
        slot = (base + lax.broadcasted_iota(jnp.int32, (rows, 1), 0)).astype(F32)
        return jnp.where(rank_e == slot, 1.0, 0.0).astype(BF16)

    def gather(base, rows):
        sel = selection(base, rows)
        xe_sc[pl.ds(base, rows), :] = _dot(sel, xn_sc[...]).astype(BF16)
        gh, gm, gl = _split3(gate_sc[...])
        gates = (_dot_nt(sel, gh) + _dot_nt(sel, gm)) + _dot_nt(sel, gl)
        lane = lax.broadcasted_iota(jnp.int32, (rows, e_rows), 1)
        ge_sc[pl.ds(base, rows), :] = jnp.sum(jnp.where(lane == e, gates, 0.0), axis=1, keepdims=True)
        ye_sc[pl.ds(base, rows), :] = jnp.zeros((rows, ye_sc.shape[1]), F32)

    def expert(base, rows):
        act = _swiglu_act(xe_sc[pl.ds(base, rows), :], wg_ref, wu_ref) * ge_sc[pl.ds(base, rows), :]
        ye_sc[pl.ds(base, rows), :] += _dot(act.astype(BF16), wd_ref[...])

    def scatter(base, rows):
        sel = selection(base, rows)
        y = ye_sc[pl.ds(base, rows), :]
        yh = y.astype(BF16)
        yl = (y - yh.astype(F32)).astype(BF16)
        acc_sc[...] += (lax.dot_general(sel, yh, _TN, preferred_element_type=F32)
                        + lax.dot_general(sel, yl, _TN, preferred_element_type=F32))

    @pl.when(j == 0)
    def _():
        for_chunks(gather)

    for_chunks(expert)

    @pl.when(j == pl.num_programs(2) - 1)
    def _():
        for_chunks(scatter)

    @pl.when((e == pl.num_programs(1) - 1) & (j == pl.num_programs(2) - 1))
    def _():
        o_ref[...] = _rmsnorm(x_ref[...] + acc_sc[...], gf_ref[...])


def _moe(x, gain, w_router_t3, w_gate_up_tiled, w_down, gain_final, tm):
    n, d = x.shape
    n_e, two_n_f, _, tf = w_gate_up_tiled.shape
    n_f = two_n_f // 2
    assert tm % MOE_CHUNK == 0
    c0 = max(MOE_CHUNK, tm * 2 // n_e)
    row_spec = pl.BlockSpec((tm, d), lambda i, e, j: (i, 0))
    vec_spec = pl.BlockSpec((1, d), lambda i, e, j: (0, 0))
    return pl.pallas_call(
        functools.partial(_moe_kernel, c0=c0),
        grid=(n // tm, n_e, n_f),
        in_specs=[row_spec, vec_spec,
                  pl.BlockSpec((N_SPLIT, LANES, d), lambda i, e, j: (0, 0, 0)),
                  pl.BlockSpec((None, None, d, tf), lambda i, e, j: (e, j, 0, 0)),
                  pl.BlockSpec((None, None, d, tf), lambda i, e, j: (e, j + n_f, 0, 0)),
                  pl.BlockSpec((None, tf, d), lambda i, e, j: (e, j, 0)),
                  vec_spec],
        out_specs=row_spec,
        out_shape=jax.ShapeDtypeStruct((n, d), F32),
        scratch_shapes=[pltpu.VMEM((tm, d), BF16), pltpu.VMEM((LANES, tm), F32), pltpu.VMEM((LANES, tm), F32),
                        pltpu.VMEM((tm, d), BF16), pltpu.VMEM((tm, 1), F32), pltpu.VMEM((tm, d), F32),
                        pltpu.VMEM((tm, d), F32)],
        compiler_params=_params("parallel", "arbitrary", "arbitrary"),
        name="moe_swiglu_final_norm",
    )(x, gain, w_router_t3, w_gate_up_tiled, w_gate_up_tiled, w_down, gain_final)


def _row_tile(n, target):
    t = min(n, target)
    while n % t:
        t //= 2
    return t


def kernel(x_prompt, x_sample, cache_k, cache_v, cache_logf, state_pool, page_table, l0_norm_attn, l0_w_qkvf, l0_b_f, l0_w_o, l0_norm_ffn, l0_w_gate_up, l0_w_down, l1_norm_pool, l1_w_group, l1_pool_scale, l1_norm_ffn, l1_w_router, l1_w_gate_up, l1_w_down, final_norm):
    b_p, s_p, d = x_prompt.shape
    b_s, t_new, _ = x_sample.shape
    assert b_p == 1 and d == N_HEADS * HEAD_DIM
    n_phys, page = cache_k.shape[:2]
    past_len = page_table.shape[1] * page
    d_ff = l0_w_down.shape[0]

    row = lambda v: v.reshape(1, -1).astype(F32)
    w_t = l0_w_qkvf.T.astype(BF16)
    wf_pad = jnp.pad(w_t[3 * d:], ((0, LANES - N_HEADS), (0, 0)))
    bf = jnp.pad(l0_b_f, (0, LANES - N_HEADS)).reshape(1, LANES)
    bft = l0_b_f.reshape(N_HEADS, 1)
    wo = l0_w_o.astype(BF16)
    tf = 512 if d_ff % 512 == 0 else d_ff
    w0_gu = _tile_gate_up(l0_w_gate_up, tf)
    w0_d = l0_w_down.astype(BF16)
    wgrp = l1_w_group.astype(BF16)
    wr_pad = jnp.pad(l1_w_router.T, ((0, LANES - N_EXPERTS), (0, 0)))
    wr_hi = wr_pad.astype(BF16)
    wr_r1 = wr_pad - wr_hi.astype(F32)
    wr_mid = wr_r1.astype(BF16)
    wr_lo = (wr_r1 - wr_mid.astype(F32)).astype(BF16)
    wr3 = jnp.stack([wr_hi, wr_mid, wr_lo])
    w1_gu = _tile_gate_up(l1_w_gate_up, tf)
    w1_d = l1_w_down.astype(BF16)

    xp = x_prompt.reshape(s_p, d)
    xs = x_sample.reshape(b_s * t_new, d)
    n_s = b_s * t_new

    blk = _row_tile(s_p, 512)
    qt_p, kh_p, kt_p, vt_p, vtb_p, lf_p, lft_p = _qkvf_prompt(xp, row(l0_norm_attn), w_t, wf_pad, bf, bft, blk)
    q_s, k_s, v_s, lf_s = _qkvf_sample(xs, row(l0_norm_attn), w_t, wf_pad, bf, _row_tile(n_s, 512))

    attn_p = _prompt_attn(qt_p, _key_aug(lf_p, kh_p, blk), vtb_p, blk)

    feat_major = lambda c: jnp.transpose(c, (0, 2, 3, 1)).reshape(n_phys, d, page)
    n_pp = 8 if page_table.shape[1] % 8 == 0 else 1
    attn_s = _sample_attn(page_table, feat_major(cache_k), feat_major(cache_v), jnp.swapaxes(cache_logf, 1, 2),
                          q_s, k_s, v_s, lf_s, n_pp)

    hp = _oproj(attn_p, xp, wo, blk, True)
    hs = _oproj(attn_s, xs, wo, _row_tile(n_s, 512), False)

    hp = _ffn(hp, row(l0_norm_ffn), w0_gu, w0_d, _row_tile(s_p, 1024))
    hs = _ffn(hs, row(l0_norm_ffn), w0_gu, w0_d, _row_tile(n_s, 1024))

    hp, tail_p = _pool_prompt(hp, row(l1_norm_pool), wgrp, row(l1_pool_scale), _row_tile(s_p, 512))
    state_pad = jnp.pad(state_pool, ((0, 0), (POOL_PAD - POOL_HIST, 0), (0, 0)))
    hs, hist_s = _pool_sample(hs, state_pad, row(l1_norm_pool), wgrp, row(l1_pool_scale), t_new, past_len)

    yp = _moe(hp, row(l1_norm_ffn), wr3, w1_gu, w1_d, row(final_norm), _row_tile(s_p, 1024))
    ys = _moe(hs, row(l1_norm_ffn), wr3, w1_gu, w1_d, row(final_norm), _row_tile(n_s, 1024))

    heads_t = lambda a: jnp.transpose(a.reshape(1, N_HEADS, HEAD_DIM, s_p), (0, 3, 1, 2))
    heads = lambda a: a.reshape(b_s, t_new, N_HEADS, HEAD_DIM)
    return (yp.reshape(b_p, s_p, d), ys.reshape(b_s, t_new, d),
            heads_t(kt_p), heads_t(vt_p), jnp.transpose(lft_p.reshape(1, N_HEADS, s_p), (0, 2, 1)),
            tail_p[POOL_PAD - POOL_HIST:].reshape(b_p, POOL_HIST, d),
            heads(k_s), heads(v_s), lf_s[:, :N_HEADS].reshape(b_s, t_new, N_HEADS),
            hist_s[:, POOL_PAD - POOL_HIST:, :])
```

```python
import functools
import math

import jax
import jax.numpy as jnp
from jax import lax
from jax.experimental import pallas as pl
from jax.experimental.pallas import tpu as pltpu

N_HEADS = 16
HEAD_DIM = 64
RMS_EPS = 1e-6
POOL_WINDOWS = (2, 4, 8, 16)
POOL_HIST = max(POOL_WINDOWS) - 1
POOL_PAD = POOL_HIST + 1
N_EXPERTS = 8
LANES = 128
N_SPLIT = 3
NEG_BIG = -1e30
LOG2_E = 1.4426950408889634
MOE_CHUNK = 128
ATTN_HEADS = 2
PAGES_PER_STEP = 16
VMEM_LIMIT_BYTES = 56 * 1024 * 1024

F32 = jnp.float32
BF16 = jnp.bfloat16
_NT = (((1,), (1,)), ((), ()))
_TN = (((0,), (0,)), ((), ()))


def _params(*sem):
    return pltpu.CompilerParams(dimension_semantics=sem, vmem_limit_bytes=VMEM_LIMIT_BYTES)


def _rmsnorm(x, gain):
    ms = jnp.mean(x * x, axis=-1, keepdims=True)
    return x * lax.rsqrt(ms + RMS_EPS) * gain


def _split3(x):
    hi = x.astype(BF16)
    r1 = x - hi.astype(F32)
    mid = r1.astype(BF16)
    lo = (r1 - mid.astype(F32)).astype(BF16)
    return hi, mid, lo


def _dot(a, b):
    return jnp.dot(a, b, preferred_element_type=F32)


def _dot_nt(a, b):
    return lax.dot_general(a, b, _NT, preferred_element_type=F32)


def _dot_exact_rhs(a_bf16, x_f32):
    hi, mid, lo = _split3(x_f32)
    return (_dot(a_bf16, hi) + _dot(a_bf16, mid)) + _dot(a_bf16, lo)


def _dot_exact_lhs(x_f32, a_bf16):
    hi, mid, lo = _split3(x_f32)
    return (_dot(hi, a_bf16) + _dot(mid, a_bf16)) + _dot(lo, a_bf16)


def _log_sigmoid(z):
    return jnp.minimum(z, 0.0) - jnp.log1p(jnp.exp(-jnp.abs(z)))


def _full_spec(shape):
    return pl.BlockSpec(shape, lambda *_: (0,) * len(shape))


def _qkvf_prompt_kernel(x_ref, g_ref, wq_ref, wk_ref, wv_ref, wf_ref, wfp_ref, bf_ref, bft_ref,
                        qt_ref, kh_ref, kt_ref, vt_ref, vtb_ref, lf_ref, lft_ref):
    xn = _rmsnorm(x_ref[...], g_ref[...]).astype(BF16)
    qt_ref[...] = (_dot_nt(wq_ref[...], xn) * (LOG2_E * HEAD_DIM ** -0.5)).astype(BF16)
    kt_ref[...] = _dot_nt(wk_ref[...], xn)
    vt = _dot_nt(wv_ref[...], xn)
    vt_ref[...] = vt
    vtb_ref[...] = vt.astype(BF16)
    k = _dot_nt(xn, wk_ref[...])
    for h in range(N_HEADS):
        kh_ref[h] = k[:, h * HEAD_DIM:(h + 1) * HEAD_DIM].astype(BF16)
    lf_ref[...] = _log_sigmoid(_dot_nt(xn, wfp_ref[...]) + bf_ref[...])
    lft_ref[...] = _log_sigmoid(_dot_nt(wf_ref[...], xn) + bft_ref[...])


def _qkvf_prompt(x, gain, w_t, wf_pad, bf, bft, tm):
    n, d = x.shape
    feat_spec = pl.BlockSpec((d, tm), lambda i: (0, i))
    w_spec = lambda blk: pl.BlockSpec((d, d), lambda i: (blk, 0))
    feat = lambda dt: jax.ShapeDtypeStruct((d, n), dt)
    return pl.pallas_call(
        _qkvf_prompt_kernel,
        grid=(n // tm,),
        in_specs=[pl.BlockSpec((tm, d), lambda i: (i, 0)), _full_spec((1, d)),
                  w_spec(0), w_spec(1), w_spec(2),
                  pl.BlockSpec((N_HEADS, d), lambda i: (3 * d // N_HEADS, 0)),
                  _full_spec((LANES, d)), _full_spec((1, LANES)), _full_spec((N_HEADS, 1))],
        out_specs=[feat_spec, pl.BlockSpec((N_HEADS, tm, HEAD_DIM), lambda i: (0, i, 0)),
                   feat_spec, feat_spec, feat_spec,
                   pl.BlockSpec((tm, LANES), lambda i: (i, 0)),
                   pl.BlockSpec((N_HEADS, tm), lambda i: (0, i))],
        out_shape=[feat(BF16), jax.ShapeDtypeStruct((N_HEADS, n, HEAD_DIM), BF16),
                   feat(F32), feat(F32), feat(BF16),
                   jax.ShapeDtypeStruct((n, LANES), F32), jax.ShapeDtypeStruct((N_HEADS, n), F32)],
        compiler_params=_params("parallel"),
        name="qkvf_proj_prompt",
    )(x, gain, w_t, w_t, w_t, w_t, wf_pad, bf, bft)


def _qkvf_sample_kernel(x_ref, g_ref, wq_ref, wk_ref, wv_ref, wfp_ref, bf_ref, q_ref, k_ref, v_ref, lf_ref):
    xn = _rmsnorm(x_ref[...], g_ref[...]).astype(BF16)
    q_ref[...] = _dot_nt(xn, wq_ref[...]) * (HEAD_DIM ** -0.5)
    k_ref[...] = _dot_nt(xn, wk_ref[...])
    v_ref[...] = _dot_nt(xn, wv_ref[...])
    lf_ref[...] = _log_sigmoid(_dot_nt(xn, wfp_ref[...]) + bf_ref[...])


def _qkvf_sample(x, gain, w_t, wf_pad, bf, tm):
    n, d = x.shape
    row_spec = pl.BlockSpec((tm, d), lambda i: (i, 0))
    w_spec = lambda blk: pl.BlockSpec((d, d), lambda i: (blk, 0))
    rows = jax.ShapeDtypeStruct((n, d), F32)
    return pl.pallas_call(
        _qkvf_sample_kernel,
        grid=(n // tm,),
        in_specs=[row_spec, _full_spec((1, d)), w_spec(0), w_spec(1), w_spec(2),
                  _full_spec((LANES, d)), _full_spec((1, LANES))],
        out_specs=[row_spec, row_spec, row_spec, pl.BlockSpec((tm, LANES), lambda i: (i, 0))],
        out_shape=[rows, rows, rows, jax.ShapeDtypeStruct((n, LANES), F32)],
        compiler_params=_params("parallel"),
        name="qkvf_proj_sample",
    )(x, gain, w_t, w_t, w_t, wf_pad, bf)


def _key_aug_kernel(lf_ref, kh_ref, ka_ref, c_ref):
    @pl.when(pl.program_id(0) == 0)
    def _():
        c_ref[...] = jnp.zeros_like(c_ref)

    t = lf_ref.shape[0]
    row = lax.broadcasted_iota(jnp.int32, (t, t), 0)
    col = lax.broadcasted_iota(jnp.int32, (t, t), 1)
    fc = _dot_exact_rhs((col <= row).astype(BF16), lf_ref[...]) + c_ref[...]
    c_ref[...] = fc[t - 1:t, :]
    lane = lax.broadcasted_iota(jnp.int32, (t, HEAD_DIM), 1)
    for h in range(N_HEADS):
        terms = _split3(jnp.broadcast_to(-LOG2_E * fc[:, h:h + 1], (t, HEAD_DIM)))
        aug = jnp.zeros((t, HEAD_DIM), F32)
        for i in range(N_SPLIT):
            aug = jnp.where(lane == i, terms[i].astype(F32), aug)
        ka_ref[h] = jnp.concatenate([kh_ref[h], aug.astype(BF16)], axis=1)


def _key_aug(lf, kh, t):
    s = lf.shape[0]
    return pl.pallas_call(
        _key_aug_kernel,
        grid=(s // t,),
        in_specs=[pl.BlockSpec((t, LANES), lambda i: (i, 0)),
                  pl.BlockSpec((N_HEADS, t, HEAD_DIM), lambda i: (0, i, 0))],
        out_specs=pl.BlockSpec((N_HEADS, t, 2 * HEAD_DIM), lambda i: (0, i, 0)),
        out_shape=jax.ShapeDtypeStruct((N_HEADS, s, 2 * HEAD_DIM), BF16),
        scratch_shapes=[pltpu.VMEM((1, LANES), F32)],
        compiler_params=_params("arbitrary"),
        name="key_aug_cumsum",
    )(lf, kh)


def _prompt_attn_kernel(qt_ref, ka_ref, vt_ref, o_ref, s_sc, mc_sc, m_sc, l_sc, acc_sc, *, blk):
    qi = pl.program_id(1)
    ones_rows = (lax.broadcasted_iota(jnp.int32, (HEAD_DIM, blk), 0) < N_SPLIT).astype(BF16)
    qa = [jnp.concatenate([qt_ref[pl.ds(hh * HEAD_DIM, HEAD_DIM), :], ones_rows], axis=0)
          for hh in range(ATTN_HEADS)]
    m_sc[...] = jnp.full_like(m_sc, NEG_BIG)
    l_sc[...] = jnp.zeros_like(l_sc)
    acc_sc[...] = jnp.zeros_like(acc_sc)
    kidx = lax.broadcasted_iota(jnp.int32, (blk, blk), 0)
    qidx = lax.broadcasted_iota(jnp.int32, (blk, blk), 1)

    def scores(ki, slot, causal):
        ks = pl.multiple_of(ki * blk, blk)
        for hh in range(ATTN_HEADS):
            s = _dot(ka_ref[hh, pl.ds(ks, blk), :], qa[hh])
            if causal:
                s = jnp.where(kidx <= qidx, s, NEG_BIG)
            s_sc[hh, slot] = s
            mc_sc[hh, slot] = jnp.max(s, axis=0, keepdims=True)

    def softmax_pv(ki, slot):
        ks = pl.multiple_of(ki * blk, blk)
        for hh in range(ATTN_HEADS):
            m_prev = m_sc[hh]
            m_new = jnp.maximum(m_prev, mc_sc[hh, slot])
            alpha = jnp.exp2(m_prev - m_new)
            p = jnp.exp2(s_sc[hh, slot] - m_new)
            l_sc[hh] = alpha * l_sc[hh] + jnp.sum(p, axis=0, keepdims=True)
            v_blk = vt_ref[pl.ds(hh * HEAD_DIM, HEAD_DIM), pl.ds(ks, blk)]
            acc_sc[hh] = alpha * acc_sc[hh] + _dot(v_blk, p.astype(BF16))
            m_sc[hh] = m_new

    @pl.when(qi == 0)
    def _():
        scores(0, 0, True)
        softmax_pv(0, 0)

    @pl.when(qi > 0)
    def _():
        scores(0, 0, False)
        n_pairs = (qi - 1) // 2

        def body(k, carry):
            scores(2 * k + 1, 1, False)
            softmax_pv(2 * k, 0)
            scores(2 * k + 2, 0, False)
            softmax_pv(2 * k + 1, 1)
            return carry

        lax.fori_loop(0, n_pairs, body, 0)
        i0 = 2 * n_pairs

        @pl.when(qi - i0 == 2)
        def _():
            scores(i0 + 1, 1, False)
            softmax_pv(i0, 0)
            scores(qi, 0, True)
            softmax_pv(i0 + 1, 1)
            softmax_pv(qi, 0)

        @pl.when(qi - i0 == 1)
        def _():
            scores(qi, 1, True)
            softmax_pv(i0, 0)
            softmax_pv(qi, 1)

    for hh in range(ATTN_HEADS):
        o_ref[pl.ds(hh * HEAD_DIM, HEAD_DIM), :] = (acc_sc[hh] / l_sc[hh]).astype(o_ref.dtype)


def _prompt_attn(qt, ka, vt, blk):
    d, s = qt.shape
    rows = ATTN_HEADS * HEAD_DIM
    return pl.pallas_call(
        functools.partial(_prompt_attn_kernel, blk=blk),
        grid=(N_HEADS // ATTN_HEADS, s // blk),
        in_specs=[pl.BlockSpec((rows, blk), lambda h, i: (h, i)),
                  pl.BlockSpec((ATTN_HEADS, s, 2 * HEAD_DIM), lambda h, i: (h, 0, 0)),
                  pl.BlockSpec((rows, s), lambda h, i: (h, 0))],
        out_specs=pl.BlockSpec((rows, blk), lambda h, i: (h, i)),
        out_shape=jax.ShapeDtypeStruct((d, s), BF16),
        scratch_shapes=[pltpu.VMEM((ATTN_HEADS, 2, blk, blk), F32), pltpu.VMEM((ATTN_HEADS, 2, 1, blk), F32),
                        pltpu.VMEM((ATTN_HEADS, 1, blk), F32), pltpu.VMEM((ATTN_HEADS, 1, blk), F32),
                        pltpu.VMEM((ATTN_HEADS, HEAD_DIM, blk), F32)],
        compiler_params=_params("parallel", "parallel"),
        name="fox_prompt_attn",
    )(qt, ka, vt)


def _sample_attn_kernel(pt_ref, *refs, n_pp, page, t_new):
    del pt_ref
    k_refs = refs[:n_pp]
    v_refs = refs[n_pp:2 * n_pp]
    lf_refs = refs[2 * n_pp:3 * n_pp]
    q_ref, kn_ref, vn_ref, lfn_ref, o_ref, qbd_sc, m_sc, l_sc, acc_sc, suf_sc = refs[3 * n_pp:]
    j = pl.program_id(1)
    rows = N_HEADS * t_new
    d = N_HEADS * HEAD_DIM
    own = (lax.broadcasted_iota(jnp.int32, (rows, d), 1) // HEAD_DIM
           == lax.broadcasted_iota(jnp.int32, (rows, d), 0) // t_new)

    def online_update(s, pv_fn):
        m_prev = m_sc[...]
        m_new = jnp.maximum(m_prev, jnp.max(s, axis=1, keepdims=True))
        alpha = jnp.exp(m_prev - m_new)
        p = jnp.exp(s - m_new)
        l_sc[...] = alpha * l_sc[...] + jnp.sum(p, axis=1, keepdims=True)
        acc_sc[...] = alpha * acc_sc[...] + pv_fn(p.astype(BF16))
        m_sc[...] = m_new

    def add_head_rows(s, bias):
        return jnp.concatenate(
            [s[h * t_new:(h + 1) * t_new, :] + bias[h:h + 1, :] for h in range(N_HEADS)], axis=0)

    @pl.when(j == 0)
    def _():
        m_sc[...] = jnp.full_like(m_sc, NEG_BIG)
        l_sc[...] = jnp.zeros_like(l_sc)
        acc_sc[...] = jnp.zeros_like(acc_sc)
        suf_sc[...] = jnp.zeros_like(suf_sc)
        q_rep = jnp.concatenate([q_ref[...]] * N_HEADS, axis=0)
        qbd_sc[...] = jnp.where(own, q_rep, 0.0).astype(BF16)
        r8 = lax.broadcasted_iota(jnp.int32, (t_new, t_new), 0)
        c8 = lax.broadcasted_iota(jnp.int32, (t_new, t_new), 1)
        cum = _dot_exact_rhs((c8 <= r8).astype(BF16), lfn_ref[...])
        cum_pad = jnp.concatenate([cum, jnp.zeros((LANES - t_new, LANES), F32)], axis=0)
        eye = (lax.broadcasted_iota(jnp.int32, (N_HEADS, LANES), 0)
               == lax.broadcasted_iota(jnp.int32, (N_HEADS, LANES), 1)).astype(BF16)
        hi, mid, lo = _split3(cum_pad)
        cum_t = (_dot_nt(eye, hi) + _dot_nt(eye, mid)) + _dot_nt(eye, lo)
        zpad = jnp.zeros((LANES - t_new, d), F32)
        kn = jnp.concatenate([kn_ref[...], zpad], axis=0).astype(BF16)
        vn = jnp.concatenate([vn_ref[...], zpad], axis=0).astype(BF16)
        s = add_head_rows(_dot_nt(qbd_sc[...], kn), -cum_t)
        qt = lax.broadcasted_iota(jnp.int32, (rows, LANES), 0) % t_new
        kt = lax.broadcasted_iota(jnp.int32, (rows, LANES), 1)
        s = jnp.where(kt <= qt, s, NEG_BIG)
        online_update(s, lambda p: _dot(p, vn))

    rp = lax.broadcasted_iota(jnp.int32, (page, page), 0)
    cp = lax.broadcasted_iota(jnp.int32, (page, page), 1)
    newer = (rp > cp).astype(BF16)
    carry = suf_sc[...]
    suf_pages = [None] * n_pp
    for i in reversed(range(n_pp)):
        lf = lf_refs[i][...]
        suf_pages[i] = _dot_exact_lhs(lf, newer) + carry
        carry = carry + jnp.sum(lf, axis=1, keepdims=True)
    suf_sc[...] = carry
    suf = jnp.concatenate(suf_pages, axis=1)

    kt_pages = jnp.concatenate([r[...] for r in k_refs], axis=1).astype(BF16)
    vt_pages = jnp.concatenate([r[...] for r in v_refs], axis=1).astype(BF16)
    s = add_head_rows(_dot(qbd_sc[...], kt_pages), suf)
    online_update(s, lambda p: _dot_nt(p, vt_pages))

    @pl.when(j == pl.num_programs(1) - 1)
    def _():
        out = jnp.where(own, acc_sc[...] / l_sc[...], 0.0)
        o = out[0:t_new, :]
        for h in range(1, N_HEADS):
            o = o + out[h * t_new:(h + 1) * t_new, :]
        o_ref[...] = o


def _sample_attn(page_table, cache_kt, cache_vt, cache_lft, q_s, k_s, v_s, lf_s, n_pp):
    n_seq, n_pages = page_table.shape
    _, d, page = cache_kt.shape
    t_new = q_s.shape[0] // n_seq
    n_grp = n_pages // n_pp

    def page_idx(i):
        return lambda b, j, pt: (pt[b, (n_grp - 1 - j) * n_pp + i], 0, 0)

    kv_specs = [pl.BlockSpec((None, d, page), page_idx(i)) for i in range(n_pp)]
    lf_specs = [pl.BlockSpec((None, N_HEADS, page), page_idx(i)) for i in range(n_pp)]
    tok_spec = pl.BlockSpec((t_new, d), lambda b, j, pt: (b, 0))
    rows = N_HEADS * t_new
    grid_spec = pltpu.PrefetchScalarGridSpec(
        num_scalar_prefetch=1,
        grid=(n_seq, n_grp),
        in_specs=kv_specs + kv_specs + lf_specs + [
            tok_spec, tok_spec, tok_spec, pl.BlockSpec((t_new, LANES), lambda b, j, pt: (b, 0))],
        out_specs=tok_spec,
        scratch_shapes=[pltpu.VMEM((rows, d), BF16), pltpu.VMEM((rows, 1), F32), pltpu.VMEM((rows, 1), F32),
                        pltpu.VMEM((rows, d), F32), pltpu.VMEM((N_HEADS, 1), F32)],
    )
    return pl.pallas_call(
        functools.partial(_sample_attn_kernel, n_pp=n_pp, page=page, t_new=t_new),
        grid_spec=grid_spec,
        out_shape=jax.ShapeDtypeStruct((n_seq * t_new, d), F32),
        compiler_params=_params("parallel", "arbitrary"),
        name="fox_sample_attn",
    )(page_table, *([cache_kt] * n_pp), *([cache_vt] * n_pp), *([cache_lft] * n_pp), q_s, k_s, v_s, lf_s)


def _oproj_kernel(a_ref, x_ref, wo_ref, o_ref, *, feature_major):
    a = a_ref[...].astype(BF16)
    if feature_major:
        proj = lax.dot_general(a, wo_ref[...], _TN, preferred_element_type=F32)
    else:
        proj = _dot(a, wo_ref[...])
    o_ref[...] = x_ref[...] + proj


def _oproj(attn, x, wo, tm, feature_major):
    n, d = x.shape
    row_spec = pl.BlockSpec((tm, d), lambda i: (i, 0))
    attn_spec = pl.BlockSpec((d, tm), lambda i: (0, i)) if feature_major else row_spec
    return pl.pallas_call(
        functools.partial(_oproj_kernel, feature_major=feature_major),
        grid=(n // tm,),
        in_specs=[attn_spec, row_spec, _full_spec((d, d))],
        out_specs=row_spec,
        out_shape=jax.ShapeDtypeStruct((n, d), F32),
        compiler_params=_params("parallel"),
        name="attn_out_proj",
    )(attn, x, wo)


def _swiglu_act(x_bf16, wg_ref, wu_ref):
    gate = _dot(x_bf16, wg_ref[...])
    return gate * jax.nn.sigmoid(gate) * _dot(x_bf16, wu_ref[...])


def _ffn_kernel(x_ref, g_ref, wg_ref, wu_ref, wd_ref, o_ref, xn_sc, acc_sc):
    j = pl.program_id(1)

    @pl.when(j == 0)
    def _():
        xn_sc[...] = _rmsnorm(x_ref[...], g_ref[...]).astype(BF16)
        acc_sc[...] = jnp.zeros_like(acc_sc)

    act = _swiglu_act(xn_sc[...], wg_ref, wu_ref).astype(BF16)
    acc_sc[...] += _dot(act, wd_ref[...])

    @pl.when(j == pl.num_programs(1) - 1)
    def _():
        o_ref[...] = x_ref[...] + acc_sc[...]


def _ffn(x, gain, w_gate_up, w_down, tm, tf):
    n, d = x.shape
    n_f = w_down.shape[0] // tf
    row_spec = pl.BlockSpec((tm, d), lambda i, j: (i, 0))
    return pl.pallas_call(
        _ffn_kernel,
        grid=(n // tm, n_f),
        in_specs=[row_spec, pl.BlockSpec((1, d), lambda i, j: (0, 0)),
                  pl.BlockSpec((d, tf), lambda i, j: (0, j)),
                  pl.BlockSpec((d, tf), lambda i, j: (0, j + n_f)),
                  pl.BlockSpec((tf, d), lambda i, j: (j, 0))],
        out_specs=row_spec,
        out_shape=jax.ShapeDtypeStruct((n, d), F32),
        scratch_shapes=[pltpu.VMEM((tm, d), BF16), pltpu.VMEM((tm, d), F32)],
        compiler_params=_params("parallel", "arbitrary"),
        name="dense_swiglu",
    )(x, gain, w_gate_up, w_gate_up, w_down)


def _pool_mix(ext_ref, x, u, pos0, wgrp_ref, scale_ref, o_ref, n_rows):
    gdim = u.shape[1] // len(POOL_WINDOWS)
    pos = pos0 + lax.broadcasted_iota(jnp.int32, (n_rows, 1), 0)
    for g, w in enumerate(POOL_WINDOWS):
        cols = pl.ds(g * gdim, gdim)
        win = ext_ref[pl.ds(POOL_PAD, n_rows), cols]
        for i in range(1, w):
            win = win + ext_ref[pl.ds(POOL_PAD - i, n_rows), cols]
        count = jnp.minimum(pos + 1, w).astype(F32)
        pooled = (win / count - u[:, g * gdim:(g + 1) * gdim]).astype(BF16)
        mixed = _dot(pooled, wgrp_ref[g]) * scale_ref[:, cols]
        o_ref[:, cols] = x[:, g * gdim:(g + 1) * gdim] + mixed


def _pool_prompt_kernel(x_ref, g_ref, wgrp_ref, scale_ref, o_ref, tail_ref, ext_sc, *, tm):
    i = pl.program_id(0)

    @pl.when(i == 0)
    def _():
        ext_sc[pl.ds(0, POOL_PAD), :] = jnp.zeros((POOL_PAD, ext_sc.shape[1]), F32)

    @pl.when(i > 0)
    def _():
        ext_sc[pl.ds(0, POOL_PAD), :] = ext_sc[pl.ds(tm, POOL_PAD), :]

    x = x_ref[...]
    u = _rmsnorm(x, g_ref[...])
    ext_sc[pl.ds(POOL_PAD, tm), :] = u
    tail_ref[...] = u[tm - POOL_PAD:, :]
    _pool_mix(ext_sc, x, u, i * tm, wgrp_ref, scale_ref, o_ref, tm)


def _pool_prompt(x, gain, w_group, scale, tm):
    n, d = x.shape
    gdim = d // len(POOL_WINDOWS)
    row_spec = pl.BlockSpec((tm, d), lambda i: (i, 0))
    return pl.pallas_call(
        functools.partial(_pool_prompt_kernel, tm=tm),
        grid=(n // tm,),
        in_specs=[row_spec, pl.BlockSpec((1, d), lambda i: (0, 0)),
                  pl.BlockSpec((len(POOL_WINDOWS), gdim, gdim), lambda i: (0, 0, 0)),
                  pl.BlockSpec((1, d), lambda i: (0, 0))],
        out_specs=[row_spec, pl.BlockSpec((POOL_PAD, d), lambda i: (0, 0))],
        out_shape=[jax.ShapeDtypeStruct((n, d), F32), jax.ShapeDtypeStruct((POOL_PAD, d), F32)],
        scratch_shapes=[pltpu.VMEM((POOL_PAD + tm, d), F32)],
        compiler_params=_params("arbitrary"),
        name="pool_mix_prompt",
    )(x, gain, w_group, scale)


def _pool_sample_kernel(x_ref, st_ref, g_ref, wgrp_ref, scale_ref, o_ref, hist_ref, ext_sc, *, t_new, past_len):
    x = x_ref[...]
    u = _rmsnorm(x, g_ref[...])
    ext_sc[pl.ds(0, POOL_PAD), :] = st_ref[...]
    ext_sc[pl.ds(POOL_PAD, t_new), :] = u
    hist_ref[...] = ext_sc[pl.ds(t_new, POOL_PAD), :]
    _pool_mix(ext_sc, x, u, past_len, wgrp_ref, scale_ref, o_ref, t_new)


def _pool_sample(x, state_pad, gain, w_group, scale, t_new, past_len):
    n, d = x.shape
    gdim = d // len(POOL_WINDOWS)
    row_spec = pl.BlockSpec((t_new, d), lambda b: (b, 0))
    st_spec = pl.BlockSpec((None, POOL_PAD, d), lambda b: (b, 0, 0))
    return pl.pallas_call(
        functools.partial(_pool_sample_kernel, t_new=t_new, past_len=past_len),
        grid=(n // t_new,),
        in_specs=[row_spec, st_spec, pl.BlockSpec((1, d), lambda b: (0, 0)),
                  pl.BlockSpec((len(POOL_WINDOWS), gdim, gdim), lambda b: (0, 0, 0)),
                  pl.BlockSpec((1, d), lambda b: (0, 0))],
        out_specs=[row_spec, st_spec],
        out_shape=[jax.ShapeDtypeStruct((n, d), F32), jax.ShapeDtypeStruct(state_pad.shape, F32)],
        scratch_shapes=[pltpu.VMEM((POOL_PAD + t_new, d), F32)],
        compiler_params=_params("parallel"),
        name="pool_mix_sample",
    )(x, state_pad, gain, w_group, scale)


def _moe_kernel(x_ref, g_ref, wrt_ref, wg_ref, wu_ref, wd_ref, gf_ref, o_ref,
                xn_sc, gate_sc, rank_sc, xe_sc, ge_sc, ye_sc, acc_sc, *, c0):
    e = pl.program_id(1)
    j = pl.program_id(2)
    tm = x_ref.shape[0]
    e_rows = gate_sc.shape[0]

    @pl.when((e == 0) & (j == 0))
    def _():
        xn = _rmsnorm(x_ref[...], g_ref[...])
        xh, xm, xl = _split3(xn)
        xn_sc[...] = xh
        wh, wm, wl = wrt_ref[0], wrt_ref[1], wrt_ref[2]
        logits = ((_dot_nt(wl, xh) + _dot_nt(wm, xm) + _dot_nt(wh, xl))
                  + (_dot_nt(wm, xh) + _dot_nt(wh, xm))) + _dot_nt(wh, xh)
        erow = lax.broadcasted_iota(jnp.int32, (e_rows, tm), 0)
        logits = jnp.where(erow < N_EXPERTS, logits, -jnp.inf)
        top1 = jnp.max(logits, axis=0, keepdims=True)
        idx1 = jnp.min(jnp.where(logits == top1, erow, e_rows), axis=0, keepdims=True)
        rest = jnp.where(erow == idx1, -jnp.inf, logits)
        top2 = jnp.max(rest, axis=0, keepdims=True)
        idx2 = jnp.min(jnp.where(rest == top2, erow, e_rows), axis=0, keepdims=True)
        e2 = jnp.exp(top2 - top1)
        denom = 1.0 + e2
        gate_sc[...] = jnp.where(erow == idx1, 1.0 / denom, 0.0) + jnp.where(erow == idx2, e2 / denom, 0.0)
        routed = (erow == idx1) | (erow == idx2)
        earlier = jnp.where(lax.broadcasted_iota(jnp.int32, (tm, tm), 0)
                            < lax.broadcasted_iota(jnp.int32, (tm, tm), 1), 1.0, 0.0).astype(BF16)
        rank = _dot(jnp.where(routed, 1.0, 0.0).astype(BF16), earlier)
        rank_sc[...] = jnp.where(routed, rank, -1.0)
        acc_sc[...] = jnp.zeros_like(acc_sc)

    rank_e = rank_sc[pl.ds(e, 1), :]
    count = jnp.sum(jnp.where(rank_e >= 0.0, 1.0, 0.0)).astype(jnp.int32)
    n_extra = lax.shift_right_logical(jnp.maximum(count - c0, 0) + (MOE_CHUNK - 1), MOE_CHUNK.bit_length() - 1)

    def for_chunks(fn):
        fn(0, c0)

        def body(k, carry):
            fn(pl.multiple_of(c0 + k * MOE_CHUNK, MOE_CHUNK), MOE_CHUNK)
            return carry

        lax.fori_loop(0, n_extra, body, 0)

    def selection(base, rows):
        slot = (base + lax.broadcasted_iota(jnp.int32, (rows, 1), 0)).astype(F32)
        return jnp.where(rank_e == slot, 1.0, 0.0).astype(BF16)

    def gather(base, rows):
        sel = selection(base, rows)
        xe_sc[pl.ds(base, rows), :] = _dot(sel, xn_sc[...]).astype(BF16)
        gh, gm, gl = _split3(gate_sc[...])
        gates = (_dot_nt(sel, gh) + _dot_nt(sel, gm)) + _dot_nt(sel, gl)
        lane = lax.broadcasted_iota(jnp.int32, (rows, e_rows), 1)
        ge_sc[pl.ds(base, rows), :] = jnp.sum(jnp.where(lane == e, gates, 0.0), axis=1, keepdims=True)
        ye_sc[pl.ds(base, rows), :] = jnp.zeros((rows, ye_sc.shape[1]), F32)

    def expert(base, rows):
        act = _swiglu_act(xe_sc[pl.ds(base, rows), :], wg_ref, wu_ref) * ge_sc[pl.ds(base, rows), :]
        ye_sc[pl.ds(base, rows), :] += _dot(act.astype(BF16), wd_ref[...])

    def scatter(base, rows):
        sel = selection(base, rows)
        y = ye_sc[pl.ds(base, rows), :]
        yh = y.astype(BF16)
        yl = (y - yh.astype(F32)).astype(BF16)
        acc_sc[...] += (lax.dot_general(sel, yh, _TN, preferred_element_type=F32)
                        + lax.dot_general(sel, yl, _TN, preferred_element_type=F32))

    @pl.when(j == 0)
    def _():
        for_chunks(gather)

    for_chunks(expert)

    @pl.when(j == pl.num_programs(2) - 1)
    def _():
        for_chunks(scatter)

    @pl.when((e == pl.num_programs(1) - 1) & (j == pl.num_programs(2) - 1))
    def _():
        o_ref[...] = _rmsnorm(x_ref[...] + acc_sc[...], gf_ref[...])


def _moe(x, gain, w_router_t3, w_gate_up, w_down, gain_final, tm, tf):
    n, d = x.shape
    n_e, d_ff, _ = w_down.shape
    n_f = d_ff // tf
    assert tm % MOE_CHUNK == 0
    c0 = max(MOE_CHUNK, tm * 2 // n_e)
    row_spec = pl.BlockSpec((tm, d), lambda i, e, j: (i, 0))
    vec_spec = pl.BlockSpec((1, d), lambda i, e, j: (0, 0))
    return pl.pallas_call(
        functools.partial(_moe_kernel, c0=c0),
        grid=(n // tm, n_e, n_f),
        in_specs=[row_spec, vec_spec,
                  pl.BlockSpec((N_SPLIT, LANES, d), lambda i, e, j: (0, 0, 0)),
                  pl.BlockSpec((None, d, tf), lambda i, e, j: (e, 0, j)),
                  pl.BlockSpec((None, d, tf), lambda i, e, j: (e, 0, j + n_f)),
                  pl.BlockSpec((None, tf, d), lambda i, e, j: (e, j, 0)),
                  vec_spec],
        out_specs=row_spec,
        out_shape=jax.ShapeDtypeStruct((n, d), F32),
        scratch_shapes=[pltpu.VMEM((tm, d), BF16), pltpu.VMEM((LANES, tm), F32), pltpu.VMEM((LANES, tm), F32),
                        pltpu.VMEM((tm, d), BF16), pltpu.VMEM((tm, 1), F32), pltpu.VMEM((tm, d), F32),
                        pltpu.VMEM((tm, d), F32)],
        compiler_params=_params("parallel", "arbitrary", "arbitrary"),
        name="moe_swiglu_final_norm",
    )(x, gain, w_router_t3, w_gate_up, w_gate_up, w_down, gain_final)


def _row_tile(n, target):
    t = min(n, target)
    while n % t:
        t //= 2
    return t


def kernel(x_prompt, x_sample, cache_k, cache_v, cache_logf, state_pool, page_table, l0_norm_attn, l0_w_qkvf, l0_b_f, l0_w_o, l0_norm_ffn, l0_w_gate_up, l0_w_down, l1_norm_pool, l1_w_group, l1_pool_scale, l1_norm_ffn, l1_w_router, l1_w_gate_up, l1_w_down, final_norm):
    b_p, s_p, d = x_prompt.shape
    b_s, t_new, _ = x_sample.shape
    assert b_p == 1 and d == N_HEADS * HEAD_DIM
    n_phys, page = cache_k.shape[:2]
    past_len = page_table.shape[1] * page
    d_ff = l0_w_down.shape[0]

    row = lambda v: v.reshape(1, -1).astype(F32)
    w_t = l0_w_qkvf.T.astype(BF16)
    wf_pad = jnp.pad(w_t[3 * d:], ((0, LANES - N_HEADS), (0, 0)))
    bf = jnp.pad(l0_b_f, (0, LANES - N_HEADS)).reshape(1, LANES)
    bft = l0_b_f.reshape(N_HEADS, 1)
    wo = l0_w_o.astype(BF16)
    tf = 512 if d_ff % 512 == 0 else d_ff
    w0_gu = l0_w_gate_up.astype(BF16)
    w0_d = l0_w_down.astype(BF16)
    wgrp = l1_w_group.astype(BF16)
    wr_pad = jnp.pad(l1_w_router.T, ((0, LANES - N_EXPERTS), (0, 0)))
    wr_hi = wr_pad.astype(BF16)
    wr_r1 = wr_pad - wr_hi.astype(F32)
    wr_mid = wr_r1.astype(BF16)
    wr_lo = (wr_r1 - wr_mid.astype(F32)).astype(BF16)
    wr3 = jnp.stack([wr_hi, wr_mid, wr_lo])
    w1_gu = l1_w_gate_up.astype(BF16)
    w1_d = l1_w_down.astype(BF16)

    xp = x_prompt.reshape(s_p, d)
    xs = x_sample.reshape(b_s * t_new, d)
    n_s = b_s * t_new

    blk = _row_tile(s_p, 512)
    qt_p, kh_p, kt_p, vt_p, vtb_p, lf_p, lft_p = _qkvf_prompt(xp, row(l0_norm_attn), w_t, wf_pad, bf, bft, blk)
    q_s, k_s, v_s, lf_s = _qkvf_sample(xs, row(l0_norm_attn), w_t, wf_pad, bf, _row_tile(n_s, 512))

    attn_p = _prompt_attn(qt_p, _key_aug(lf_p, kh_p, blk), vtb_p, blk)

    feat_major = lambda c: jnp.transpose(c, (0, 2, 3, 1)).reshape(n_phys, d, page)
    n_pp = math.gcd(page_table.shape[1], PAGES_PER_STEP)
    attn_s = _sample_attn(page_table, feat_major(cache_k), feat_major(cache_v), jnp.swapaxes(cache_logf, 1, 2),
                          q_s, k_s, v_s, lf_s, n_pp)

    hp = _oproj(attn_p, xp, wo, blk, True)
    hs = _oproj(attn_s, xs, wo, _row_tile(n_s, 512), False)

    hp = _ffn(hp, row(l0_norm_ffn), w0_gu, w0_d, _row_tile(s_p, 1024), tf)
    hs = _ffn(hs, row(l0_norm_ffn), w0_gu, w0_d, _row_tile(n_s, 1024), tf)

    hp, tail_p = _pool_prompt(hp, row(l1_norm_pool), wgrp, row(l1_pool_scale), _row_tile(s_p, 512))
    state_pad = jnp.pad(state_pool, ((0, 0), (POOL_PAD - POOL_HIST, 0), (0, 0)))
    hs, hist_s = _pool_sample(hs, state_pad, row(l1_norm_pool), wgrp, row(l1_pool_scale), t_new, past_len)

    yp = _moe(hp, row(l1_norm_ffn), wr3, w1_gu, w1_d, row(final_norm), _row_tile(s_p, 1024), tf)
    ys = _moe(hs, row(l1_norm_ffn), wr3, w1_gu, w1_d, row(final_norm), _row_tile(n_s, 1024), tf)

    heads_t = lambda a: jnp.transpose(a.reshape(1, N_HEADS, HEAD_DIM, s_p), (0, 3, 1, 2))
    heads = lambda a: a.reshape(b_s, t_new, N_HEADS, HEAD_DIM)
    return (yp.reshape(b_p, s_p, d), ys.reshape(b_s, t_new, d),
            heads_t(kt_p), heads_t(vt_p), jnp.transpose(lft_p.reshape(1, N_HEADS, s_p), (0, 2, 1)),
            tail_p[POOL_PAD - POOL_HIST:].reshape(b_p, POOL_HIST, d),
            heads(k_s), heads(v_s), lf_s[:, :N_HEADS].reshape(b_s, t_new, N_HEADS),
            hist_s[:, POOL_PAD - POOL_HIST:, :])
```

```python
import functools
import math

import jax
import jax.numpy as jnp
from jax import lax
from jax.experimental import pallas as pl
from jax.experimental.pallas import tpu as pltpu

N_HEADS = 16
HEAD_DIM = 64
RMS_EPS = 1e-6
POOL_WINDOWS = (2, 4, 8, 16)
POOL_HIST = max(POOL_WINDOWS) - 1
POOL_PAD = POOL_HIST + 1
N_EXPERTS = 8
LANES = 128
BF16_SUBLANES = 16
N_SPLIT = 3
NEG_BIG = -1e30
LOG2_E = 1.4426950408889634
MOE_CHUNK = 128
ATTN_HEADS = 2
ATTN_BLOCK = 512
PAGES_PER_STEP = 16
VMEM_LIMIT_BYTES = 56 * 1024 * 1024

F32 = jnp.float32
BF16 = jnp.bfloat16
_NT = (((1,), (1,)), ((), ()))
_TN = (((0,), (0,)), ((), ()))


def _params(*sem):
    return pltpu.CompilerParams(dimension_semantics=sem, vmem_limit_bytes=VMEM_LIMIT_BYTES)


def _rmsnorm(x, gain):
    ms = jnp.mean(x * x, axis=-1, keepdims=True)
    return x * lax.rsqrt(ms + RMS_EPS) * gain


def _split3(x):
    hi = x.astype(BF16)
    r1 = x - hi.astype(F32)
    mid = r1.astype(BF16)
    lo = (r1 - mid.astype(F32)).astype(BF16)
    return hi, mid, lo


def _dot(a, b):
    return jnp.dot(a, b, preferred_element_type=F32)


def _dot_nt(a, b):
    return lax.dot_general(a, b, _NT, preferred_element_type=F32)


def _dot_exact_rhs(a_bf16, x_f32):
    hi, mid, lo = _split3(x_f32)
    return (_dot(a_bf16, hi) + _dot(a_bf16, mid)) + _dot(a_bf16, lo)


def _dot_exact_lhs(x_f32, a_bf16):
    hi, mid, lo = _split3(x_f32)
    return (_dot(hi, a_bf16) + _dot(mid, a_bf16)) + _dot(lo, a_bf16)


def _log_sigmoid(z):
    return jnp.minimum(z, 0.0) - jnp.log1p(jnp.exp(-jnp.abs(z)))


def _full_spec(shape):
    return pl.BlockSpec(shape, lambda *_: (0,) * len(shape))


def _qkvf_prompt_kernel(x_ref, g_ref, wq_ref, wk_ref, wv_ref, wf_ref, wfp_ref, bf_ref, bft_ref,
                        qt_ref, kh_ref, kt_ref, vt_ref, vtb_ref, lf_ref, lft_ref):
    xn = _rmsnorm(x_ref[...], g_ref[...]).astype(BF16)
    qt_ref[...] = (_dot_nt(wq_ref[...], xn) * (LOG2_E * HEAD_DIM ** -0.5)).astype(BF16)
    kt_ref[...] = _dot_nt(wk_ref[...], xn)
    vt = _dot_nt(wv_ref[...], xn)
    vt_ref[...] = vt
    vtb_ref[...] = vt.astype(BF16)
    k = _dot_nt(xn, wk_ref[...])
    for h in range(N_HEADS):
        kh_ref[h] = k[:, h * HEAD_DIM:(h + 1) * HEAD_DIM].astype(BF16)
    lf_ref[...] = _log_sigmoid(_dot_nt(xn, wfp_ref[...]) + bf_ref[...])
    lft_ref[...] = _log_sigmoid(_dot_nt(wf_ref[...], xn) + bft_ref[...])


def _qkvf_prompt(x, gain, w_t, wf_pad, bf, bft, tm):
    n, d = x.shape
    feat_spec = pl.BlockSpec((d, tm), lambda i: (0, i))
    w_spec = lambda blk: pl.BlockSpec((d, d), lambda i: (blk, 0))
    feat = lambda dt: jax.ShapeDtypeStruct((d, n), dt)
    return pl.pallas_call(
        _qkvf_prompt_kernel,
        grid=(n // tm,),
        in_specs=[pl.BlockSpec((tm, d), lambda i: (i, 0)), _full_spec((1, d)),
                  w_spec(0), w_spec(1), w_spec(2),
                  pl.BlockSpec((N_HEADS, d), lambda i: (3 * d // N_HEADS, 0)),
                  _full_spec((LANES, d)), _full_spec((1, LANES)), _full_spec((N_HEADS, 1))],
        out_specs=[feat_spec, pl.BlockSpec((N_HEADS, tm, HEAD_DIM), lambda i: (0, i, 0)),
                   feat_spec, feat_spec, feat_spec,
                   pl.BlockSpec((tm, LANES), lambda i: (i, 0)),
                   pl.BlockSpec((N_HEADS, tm), lambda i: (0, i))],
        out_shape=[feat(BF16), jax.ShapeDtypeStruct((N_HEADS, n, HEAD_DIM), BF16),
                   feat(F32), feat(F32), feat(BF16),
                   jax.ShapeDtypeStruct((n, LANES), F32), jax.ShapeDtypeStruct((N_HEADS, n), F32)],
        compiler_params=_params("parallel"),
        name="qkvf_proj_prompt",
    )(x, gain, w_t, w_t, w_t, w_t, wf_pad, bf, bft)


def _qkvf_sample_kernel(x_ref, g_ref, wq_ref, wk_ref, wv_ref, wfp_ref, bf_ref, q_ref, k_ref, v_ref, lf_ref):
    xn = _rmsnorm(x_ref[...], g_ref[...]).astype(BF16)
    q_ref[...] = _dot_nt(xn, wq_ref[...]) * (HEAD_DIM ** -0.5)
    k_ref[...] = _dot_nt(xn, wk_ref[...])
    v_ref[...] = _dot_nt(xn, wv_ref[...])
    lf_ref[...] = _log_sigmoid(_dot_nt(xn, wfp_ref[...]) + bf_ref[...])


def _qkvf_sample(x, gain, w_t, wf_pad, bf, tm):
    n, d = x.shape
    row_spec = pl.BlockSpec((tm, d), lambda i: (i, 0))
    w_spec = lambda blk: pl.BlockSpec((d, d), lambda i: (blk, 0))
    rows = jax.ShapeDtypeStruct((n, d), F32)
    return pl.pallas_call(
        _qkvf_sample_kernel,
        grid=(n // tm,),
        in_specs=[row_spec, _full_spec((1, d)), w_spec(0), w_spec(1), w_spec(2),
                  _full_spec((LANES, d)), _full_spec((1, LANES))],
        out_specs=[row_spec, row_spec, row_spec, pl.BlockSpec((tm, LANES), lambda i: (i, 0))],
        out_shape=[rows, rows, rows, jax.ShapeDtypeStruct((n, LANES), F32)],
        compiler_params=_params("parallel"),
        name="qkvf_proj_sample",
    )(x, gain, w_t, w_t, w_t, wf_pad, bf)


def _key_aug_kernel(lf_ref, kh_ref, ka_ref, c_ref):
    @pl.when(pl.program_id(0) == 0)
    def _():
        c_ref[...] = jnp.zeros_like(c_ref)

    t = lf_ref.shape[0]
    row = lax.broadcasted_iota(jnp.int32, (t, t), 0)
    col = lax.broadcasted_iota(jnp.int32, (t, t), 1)
    fc = _dot_exact_rhs((col <= row).astype(BF16), lf_ref[...]) + c_ref[...]
    c_ref[...] = fc[t - 1:t, :]
    lane = lax.broadcasted_iota(jnp.int32, (t, HEAD_DIM), 1)
    for h in range(N_HEADS):
        terms = _split3(jnp.broadcast_to(-LOG2_E * fc[:, h:h + 1], (t, HEAD_DIM)))
        aug = jnp.zeros((t, HEAD_DIM), F32)
        for i in range(N_SPLIT):
            aug = jnp.where(lane == i, terms[i].astype(F32), aug)
        ka_ref[h] = jnp.concatenate([kh_ref[h], aug.astype(BF16)], axis=1)


def _key_aug(lf, kh, t):
    s = lf.shape[0]
    return pl.pallas_call(
        _key_aug_kernel,
        grid=(s // t,),
        in_specs=[pl.BlockSpec((t, LANES), lambda i: (i, 0)),
                  pl.BlockSpec((N_HEADS, t, HEAD_DIM), lambda i: (0, i, 0))],
        out_specs=pl.BlockSpec((N_HEADS, t, 2 * HEAD_DIM), lambda i: (0, i, 0)),
        out_shape=jax.ShapeDtypeStruct((N_HEADS, s, 2 * HEAD_DIM), BF16),
        scratch_shapes=[pltpu.VMEM((1, LANES), F32)],
        compiler_params=_params("arbitrary"),
        name="key_aug_cumsum",
    )(lf, kh)


def _prompt_attn_kernel(qt_ref, ka_ref, vt_ref, o_ref, s_sc, mc_sc, m_sc, l_sc, acc_sc, *, blk):
    qi = pl.program_id(1)
    ones_rows = (lax.broadcasted_iota(jnp.int32, (HEAD_DIM, blk), 0) < N_SPLIT).astype(BF16)
    qa = [jnp.concatenate([qt_ref[pl.ds(hh * HEAD_DIM, HEAD_DIM), :], ones_rows], axis=0)
          for hh in range(ATTN_HEADS)]
    m_sc[...] = jnp.full_like(m_sc, NEG_BIG)
    l_sc[...] = jnp.zeros_like(l_sc)
    acc_sc[...] = jnp.zeros_like(acc_sc)
    kidx = lax.broadcasted_iota(jnp.int32, (blk, blk), 0)
    qidx = lax.broadcasted_iota(jnp.int32, (blk, blk), 1)

    def scores(ki, slot, causal):
        ks = pl.multiple_of(ki * blk, blk)
        for hh in range(ATTN_HEADS):
            s = _dot(ka_ref[hh, pl.ds(ks, blk), :], qa[hh])
            if causal:
                s = jnp.where(kidx <= qidx, s, NEG_BIG)
            s_sc[hh, slot] = s
            mc_sc[hh, slot] = jnp.max(s, axis=0, keepdims=True)

    def softmax_pv(ki, slot):
        ks = pl.multiple_of(ki * blk, blk)
        for hh in range(ATTN_HEADS):
            m_prev = m_sc[hh]
            m_new = jnp.maximum(m_prev, mc_sc[hh, slot])
            alpha = jnp.exp2(m_prev - m_new)
            p = jnp.exp2(s_sc[hh, slot] - m_new)
            l_sc[hh] = alpha * l_sc[hh] + jnp.sum(p, axis=0, keepdims=True)
            v_blk = vt_ref[pl.ds(hh * HEAD_DIM, HEAD_DIM), pl.ds(ks, blk)]
            acc_sc[hh] = alpha * acc_sc[hh] + _dot(v_blk, p.astype(BF16))
            m_sc[hh] = m_new

    @pl.when(qi == 0)
    def _():
        scores(0, 0, True)
        softmax_pv(0, 0)

    @pl.when(qi > 0)
    def _():
        scores(0, 0, False)
        n_pairs = (qi - 1) // 2

        def body(k, carry):
            scores(2 * k + 1, 1, False)
            softmax_pv(2 * k, 0)
            scores(2 * k + 2, 0, False)
            softmax_pv(2 * k + 1, 1)
            return carry

        lax.fori_loop(0, n_pairs, body, 0)
        i0 = 2 * n_pairs

        @pl.when(qi - i0 == 2)
        def _():
            scores(i0 + 1, 1, False)
            softmax_pv(i0, 0)
            scores(qi, 0, True)
            softmax_pv(i0 + 1, 1)
            softmax_pv(qi, 0)

        @pl.when(qi - i0 == 1)
        def _():
            scores(qi, 1, True)
            softmax_pv(i0, 0)
            softmax_pv(qi, 1)

    for hh in range(ATTN_HEADS):
        o_ref[pl.ds(hh * HEAD_DIM, HEAD_DIM), :] = (acc_sc[hh] / l_sc[hh]).astype(o_ref.dtype)


def _prompt_attn(qt, ka, vt, blk):
    d, s = qt.shape
    rows = ATTN_HEADS * HEAD_DIM
    return pl.pallas_call(
        functools.partial(_prompt_attn_kernel, blk=blk),
        grid=(N_HEADS // ATTN_HEADS, s // blk),
        in_specs=[pl.BlockSpec((rows, blk), lambda h, i: (h, i)),
                  pl.BlockSpec((ATTN_HEADS, s, 2 * HEAD_DIM), lambda h, i: (h, 0, 0)),
                  pl.BlockSpec((rows, s), lambda h, i: (h, 0))],
        out_specs=pl.BlockSpec((rows, blk), lambda h, i: (h, i)),
        out_shape=jax.ShapeDtypeStruct((d, s), BF16),
        scratch_shapes=[pltpu.VMEM((ATTN_HEADS, 2, blk, blk), F32), pltpu.VMEM((ATTN_HEADS, 2, 1, blk), F32),
                        pltpu.VMEM((ATTN_HEADS, 1, blk), F32), pltpu.VMEM((ATTN_HEADS, 1, blk), F32),
                        pltpu.VMEM((ATTN_HEADS, HEAD_DIM, blk), F32)],
        compiler_params=_params("parallel", "parallel"),
        name="fox_prompt_attn",
    )(qt, ka, vt)


def _sample_attn_kernel(pt_ref, *refs, n_pp, page, t_new):
    del pt_ref
    k_refs = refs[:n_pp]
    v_refs = refs[n_pp:2 * n_pp]
    lf_refs = refs[2 * n_pp:3 * n_pp]
    q_ref, kn_ref, vn_ref, lfn_ref, o_ref, qbd_sc, m_sc, l_sc, acc_sc, suf_sc = refs[3 * n_pp:]
    j = pl.program_id(1)
    rows = N_HEADS * t_new
    d = N_HEADS * HEAD_DIM
    own = (lax.broadcasted_iota(jnp.int32, (rows, d), 1) // HEAD_DIM
           == lax.broadcasted_iota(jnp.int32, (rows, d), 0) // t_new)

    def online_update(s, pv_fn):
        m_prev = m_sc[...]
        m_new = jnp.maximum(m_prev, jnp.max(s, axis=1, keepdims=True))
        alpha = jnp.exp(m_prev - m_new)
        p = jnp.exp(s - m_new)
        l_sc[...] = alpha * l_sc[...] + jnp.sum(p, axis=1, keepdims=True)
        acc_sc[...] = alpha * acc_sc[...] + pv_fn(p.astype(BF16))
        m_sc[...] = m_new

    def add_head_rows(s, bias):
        return jnp.concatenate(
            [s[h * t_new:(h + 1) * t_new, :] + bias[h:h + 1, :] for h in range(N_HEADS)], axis=0)

    @pl.when(j == 0)
    def _():
        m_sc[...] = jnp.full_like(m_sc, NEG_BIG)
        l_sc[...] = jnp.zeros_like(l_sc)
        acc_sc[...] = jnp.zeros_like(acc_sc)
        suf_sc[...] = jnp.zeros_like(suf_sc)
        q_rep = jnp.concatenate([q_ref[...]] * N_HEADS, axis=0)
        qbd_sc[...] = jnp.where(own, q_rep, 0.0).astype(BF16)
        r8 = lax.broadcasted_iota(jnp.int32, (t_new, t_new), 0)
        c8 = lax.broadcasted_iota(jnp.int32, (t_new, t_new), 1)
        cum = _dot_exact_rhs((c8 <= r8).astype(BF16), lfn_ref[...])
        cum_pad = jnp.concatenate([cum, jnp.zeros((LANES - t_new, LANES), F32)], axis=0)
        eye = (lax.broadcasted_iota(jnp.int32, (N_HEADS, LANES), 0)
               == lax.broadcasted_iota(jnp.int32, (N_HEADS, LANES), 1)).astype(BF16)
        hi, mid, lo = _split3(cum_pad)
        cum_t = (_dot_nt(eye, hi) + _dot_nt(eye, mid)) + _dot_nt(eye, lo)
        zpad = jnp.zeros((LANES - t_new, d), F32)
        kn = jnp.concatenate([kn_ref[...], zpad], axis=0).astype(BF16)
        vn = jnp.concatenate([vn_ref[...], zpad], axis=0).astype(BF16)
        s = add_head_rows(_dot_nt(qbd_sc[...], kn), -cum_t)
        qt = lax.broadcasted_iota(jnp.int32, (rows, LANES), 0) % t_new
        kt = lax.broadcasted_iota(jnp.int32, (rows, LANES), 1)
        s = jnp.where(kt <= qt, s, NEG_BIG)
        online_update(s, lambda p: _dot(p, vn))

    rp = lax.broadcasted_iota(jnp.int32, (page, page), 0)
    cp = lax.broadcasted_iota(jnp.int32, (page, page), 1)
    newer = (rp > cp).astype(BF16)
    carry = suf_sc[...]
    suf_pages = [None] * n_pp
    for i in reversed(range(n_pp)):
        lf = lf_refs[i][...]
        suf_pages[i] = _dot_exact_lhs(lf, newer) + carry
        carry = carry + jnp.sum(lf, axis=1, keepdims=True)
    suf_sc[...] = carry
    suf = jnp.concatenate(suf_pages, axis=1)

    kt_pages = jnp.concatenate([r[...] for r in k_refs], axis=1).astype(BF16)
    vt_pages = jnp.concatenate([r[...] for r in v_refs], axis=1).astype(BF16)
    s = add_head_rows(_dot(qbd_sc[...], kt_pages), suf)
    online_update(s, lambda p: _dot_nt(p, vt_pages))

    @pl.when(j == pl.num_programs(1) - 1)
    def _():
        out = jnp.where(own, acc_sc[...] / l_sc[...], 0.0)
        o = out[0:t_new, :]
        for h in range(1, N_HEADS):
            o = o + out[h * t_new:(h + 1) * t_new, :]
        o_ref[...] = o


def _sample_attn(page_table, cache_kt, cache_vt, cache_lft, q_s, k_s, v_s, lf_s, n_pp):
    n_seq, n_pages = page_table.shape
    _, d, page = cache_kt.shape
    t_new = q_s.shape[0] // n_seq
    n_grp = n_pages // n_pp

    def page_idx(i):
        return lambda b, j, pt: (pt[b, (n_grp - 1 - j) * n_pp + i], 0, 0)

    kv_specs = [pl.BlockSpec((None, d, page), page_idx(i)) for i in range(n_pp)]
    lf_specs = [pl.BlockSpec((None, N_HEADS, page), page_idx(i)) for i in range(n_pp)]
    tok_spec = pl.BlockSpec((t_new, d), lambda b, j, pt: (b, 0))
    rows = N_HEADS * t_new
    grid_spec = pltpu.PrefetchScalarGridSpec(
        num_scalar_prefetch=1,
        grid=(n_seq, n_grp),
        in_specs=kv_specs + kv_specs + lf_specs + [
            tok_spec, tok_spec, tok_spec, pl.BlockSpec((t_new, LANES), lambda b, j, pt: (b, 0))],
        out_specs=tok_spec,
        scratch_shapes=[pltpu.VMEM((rows, d), BF16), pltpu.VMEM((rows, 1), F32), pltpu.VMEM((rows, 1), F32),
                        pltpu.VMEM((rows, d), F32), pltpu.VMEM((N_HEADS, 1), F32)],
    )
    return pl.pallas_call(
        functools.partial(_sample_attn_kernel, n_pp=n_pp, page=page, t_new=t_new),
        grid_spec=grid_spec,
        out_shape=jax.ShapeDtypeStruct((n_seq * t_new, d), F32),
        compiler_params=_params("parallel", "arbitrary"),
        name="fox_sample_attn",
    )(page_table, *([cache_kt] * n_pp), *([cache_vt] * n_pp), *([cache_lft] * n_pp), q_s, k_s, v_s, lf_s)


def _oproj_kernel(a_ref, x_ref, wo_ref, o_ref, *, feature_major):
    a = a_ref[...].astype(BF16)
    if feature_major:
        proj = lax.dot_general(a, wo_ref[...], _TN, preferred_element_type=F32)
    else:
        proj = _dot(a, wo_ref[...])
    o_ref[...] = x_ref[...] + proj


def _oproj(attn, x, wo, tm, feature_major):
    n, d = x.shape
    row_spec = pl.BlockSpec((tm, d), lambda i: (i, 0))
    attn_spec = pl.BlockSpec((d, tm), lambda i: (0, i)) if feature_major else row_spec
    return pl.pallas_call(
        functools.partial(_oproj_kernel, feature_major=feature_major),
        grid=(n // tm,),
        in_specs=[attn_spec, row_spec, _full_spec((d, d))],
        out_specs=row_spec,
        out_shape=jax.ShapeDtypeStruct((n, d), F32),
        compiler_params=_params("parallel"),
        name="attn_out_proj",
    )(attn, x, wo)


def _swiglu_act(x_bf16, wg_ref, wu_ref):
    gate = _dot(x_bf16, wg_ref[...])
    return gate * jax.nn.sigmoid(gate) * _dot(x_bf16, wu_ref[...])


def _ffn_kernel(x_ref, g_ref, wg_ref, wu_ref, wd_ref, o_ref, xn_sc, acc_sc):
    j = pl.program_id(1)

    @pl.when(j == 0)
    def _():
        xn_sc[...] = _rmsnorm(x_ref[...], g_ref[...]).astype(BF16)
        acc_sc[...] = jnp.zeros_like(acc_sc)

    act = _swiglu_act(xn_sc[...], wg_ref, wu_ref).astype(BF16)
    acc_sc[...] += _dot(act, wd_ref[...])

    @pl.when(j == pl.num_programs(1) - 1)
    def _():
        o_ref[...] = x_ref[...] + acc_sc[...]


def _ffn(x, gain, w_gate_up, w_down, tm, tf):
    n, d = x.shape
    n_f = w_down.shape[0] // tf
    row_spec = pl.BlockSpec((tm, d), lambda i, j: (i, 0))
    return pl.pallas_call(
        _ffn_kernel,
        grid=(n // tm, n_f),
        in_specs=[row_spec, pl.BlockSpec((1, d), lambda i, j: (0, 0)),
                  pl.BlockSpec((d, tf), lambda i, j: (0, j)),
                  pl.BlockSpec((d, tf), lambda i, j: (0, j + n_f)),
                  pl.BlockSpec((tf, d), lambda i, j: (j, 0))],
        out_specs=row_spec,
        out_shape=jax.ShapeDtypeStruct((n, d), F32),
        scratch_shapes=[pltpu.VMEM((tm, d), BF16), pltpu.VMEM((tm, d), F32)],
        compiler_params=_params("parallel", "arbitrary"),
        name="dense_swiglu",
    )(x, gain, w_gate_up, w_gate_up, w_down)


def _pool_mix(ext_ref, x, u, pos0, wgrp_ref, scale_ref, o_ref, n_rows):
    gdim = u.shape[1] // len(POOL_WINDOWS)
    pos = pos0 + lax.broadcasted_iota(jnp.int32, (n_rows, 1), 0)
    for g, w in enumerate(POOL_WINDOWS):
        cols = pl.ds(g * gdim, gdim)
        win = ext_ref[pl.ds(POOL_PAD, n_rows), cols]
        for i in range(1, w):
            win = win + ext_ref[pl.ds(POOL_PAD - i, n_rows), cols]
        count = jnp.minimum(pos + 1, w).astype(F32)
        pooled = (win / count - u[:, g * gdim:(g + 1) * gdim]).astype(BF16)
        mixed = _dot(pooled, wgrp_ref[g]) * scale_ref[:, cols]
        o_ref[:, cols] = x[:, g * gdim:(g + 1) * gdim] + mixed


def _pool_prompt_kernel(x_ref, g_ref, wgrp_ref, scale_ref, o_ref, tail_ref, ext_sc, *, tm):
    i = pl.program_id(0)

    @pl.when(i == 0)
    def _():
        ext_sc[pl.ds(0, POOL_PAD), :] = jnp.zeros((POOL_PAD, ext_sc.shape[1]), F32)

    @pl.when(i > 0)
    def _():
        ext_sc[pl.ds(0, POOL_PAD), :] = ext_sc[pl.ds(tm, POOL_PAD), :]

    x = x_ref[...]
    u = _rmsnorm(x, g_ref[...])
    ext_sc[pl.ds(POOL_PAD, tm), :] = u
    tail_ref[...] = u[tm - POOL_PAD:, :]
    _pool_mix(ext_sc, x, u, i * tm, wgrp_ref, scale_ref, o_ref, tm)


def _pool_prompt(x, gain, w_group, scale, tm):
    n, d = x.shape
    gdim = d // len(POOL_WINDOWS)
    row_spec = pl.BlockSpec((tm, d), lambda i: (i, 0))
    return pl.pallas_call(
        functools.partial(_pool_prompt_kernel, tm=tm),
        grid=(n // tm,),
        in_specs=[row_spec, pl.BlockSpec((1, d), lambda i: (0, 0)),
                  pl.BlockSpec((len(POOL_WINDOWS), gdim, gdim), lambda i: (0, 0, 0)),
                  pl.BlockSpec((1, d), lambda i: (0, 0))],
        out_specs=[row_spec, pl.BlockSpec((POOL_PAD, d), lambda i: (0, 0))],
        out_shape=[jax.ShapeDtypeStruct((n, d), F32), jax.ShapeDtypeStruct((POOL_PAD, d), F32)],
        scratch_shapes=[pltpu.VMEM((POOL_PAD + tm, d), F32)],
        compiler_params=_params("arbitrary"),
        name="pool_mix_prompt",
    )(x, gain, w_group, scale)


def _pool_sample_kernel(x_ref, st_ref, g_ref, wgrp_ref, scale_ref, o_ref, hist_ref, ext_sc, *, t_new, past_len):
    x = x_ref[...]
    u = _rmsnorm(x, g_ref[...])
    ext_sc[pl.ds(0, POOL_PAD), :] = st_ref[...]
    ext_sc[pl.ds(POOL_PAD, t_new), :] = u
    hist_ref[...] = ext_sc[pl.ds(t_new, POOL_PAD), :]
    _pool_mix(ext_sc, x, u, past_len, wgrp_ref, scale_ref, o_ref, t_new)


def _pool_sample(x, state_pad, gain, w_group, scale, t_new, past_len):
    n, d = x.shape
    gdim = d // len(POOL_WINDOWS)
    row_spec = pl.BlockSpec((t_new, d), lambda b: (b, 0))
    st_spec = pl.BlockSpec((None, POOL_PAD, d), lambda b: (b, 0, 0))
    return pl.pallas_call(
        functools.partial(_pool_sample_kernel, t_new=t_new, past_len=past_len),
        grid=(n // t_new,),
        in_specs=[row_spec, st_spec, pl.BlockSpec((1, d), lambda b: (0, 0)),
                  pl.BlockSpec((len(POOL_WINDOWS), gdim, gdim), lambda b: (0, 0, 0)),
                  pl.BlockSpec((1, d), lambda b: (0, 0))],
        out_specs=[row_spec, st_spec],
        out_shape=[jax.ShapeDtypeStruct((n, d), F32), jax.ShapeDtypeStruct(state_pad.shape, F32)],
        scratch_shapes=[pltpu.VMEM((POOL_PAD + t_new, d), F32)],
        compiler_params=_params("parallel"),
        name="pool_mix_sample",
    )(x, state_pad, gain, w_group, scale)


def _moe_kernel(x_ref, g_ref, wrt_ref, wg_ref, wu_ref, wd_ref, gf_ref, o_ref,
                xn_sc, gate_sc, rank_sc, xe_sc, ge_sc, ye_sc, acc_sc, *, c0):
    e = pl.program_id(1)
    j = pl.program_id(2)
    tm = x_ref.shape[0]
    e_rows = gate_sc.shape[0]

    @pl.when((e == 0) & (j == 0))
    def _():
        xn = _rmsnorm(x_ref[...], g_ref[...])
        xh, xm, xl = _split3(xn)
        xn_sc[...] = xh
        wh, wm, wl = wrt_ref[0], wrt_ref[1], wrt_ref[2]
        logits = ((_dot_nt(wl, xh) + _dot_nt(wm, xm) + _dot_nt(wh, xl))
                  + (_dot_nt(wm, xh) + _dot_nt(wh, xm))) + _dot_nt(wh, xh)
        erow = lax.broadcasted_iota(jnp.int32, (e_rows, tm), 0)
        logits = jnp.where(erow < N_EXPERTS, logits, -jnp.inf)
        top1 = jnp.max(logits, axis=0, keepdims=True)
        idx1 = jnp.min(jnp.where(logits == top1, erow, e_rows), axis=0, keepdims=True)
        rest = jnp.where(erow == idx1, -jnp.inf, logits)
        top2 = jnp.max(rest, axis=0, keepdims=True)
        idx2 = jnp.min(jnp.where(rest == top2, erow, e_rows), axis=0, keepdims=True)
        e2 = jnp.exp(top2 - top1)
        denom = 1.0 + e2
        gate_sc[...] = jnp.where(erow == idx1, 1.0 / denom, 0.0) + jnp.where(erow == idx2, e2 / denom, 0.0)
        routed = (erow == idx1) | (erow == idx2)
        earlier = jnp.where(lax.broadcasted_iota(jnp.int32, (tm, tm), 0)
                            < lax.broadcasted_iota(jnp.int32, (tm, tm), 1), 1.0, 0.0).astype(BF16)
        rank = _dot(jnp.where(routed, 1.0, 0.0).astype(BF16), earlier)
        rank_sc[...] = jnp.where(routed, rank, -1.0)
        acc_sc[...] = jnp.zeros_like(acc_sc)

    rank_e = rank_sc[pl.ds(e, 1), :]
    count = jnp.sum(jnp.where(rank_e >= 0.0, 1.0, 0.0)).astype(jnp.int32)
    n_extra = lax.shift_right_logical(jnp.maximum(count - c0, 0) + (MOE_CHUNK - 1), MOE_CHUNK.bit_length() - 1)

    def for_chunks(fn):
        fn(0, c0)

        def body(k, carry):
            fn(pl.multiple_of(c0 + k * MOE_CHUNK, BF16_SUBLANES), MOE_CHUNK)
            return carry

        lax.fori_loop(0, n_extra, body, 0)

    def selected(base, rows, value):
        slot = (base + lax.broadcasted_iota(jnp.int32, (rows, 1), 0)).astype(F32)
        return jnp.where(rank_e == slot, value, 0.0)

    def selection(base, rows):
        return selected(base, rows, 1.0).astype(BF16)

    def gather(base, rows):
        xe_sc[pl.ds(base, rows), :] = _dot(selection(base, rows), xn_sc[...]).astype(BF16)
        gate_e = gate_sc[pl.ds(e, 1), :]
        ge_sc[pl.ds(base, rows), :] = jnp.sum(selected(base, rows, gate_e), axis=1, keepdims=True)
        ye_sc[pl.ds(base, rows), :] = jnp.zeros((rows, ye_sc.shape[1]), F32)

    def expert(base, rows):
        act = _swiglu_act(xe_sc[pl.ds(base, rows), :], wg_ref, wu_ref) * ge_sc[pl.ds(base, rows), :]
        ye_sc[pl.ds(base, rows), :] += _dot(act.astype(BF16), wd_ref[...])

    def scatter(base, rows):
        y = ye_sc[pl.ds(base, rows), :].astype(BF16)
        acc_sc[...] += lax.dot_general(selection(base, rows), y, _TN, preferred_element_type=F32)

    @pl.when(j == 0)
    def _():
        for_chunks(gather)

    for_chunks(expert)

    @pl.when(j == pl.num_programs(2) - 1)
    def _():
        for_chunks(scatter)

    @pl.when((e == pl.num_programs(1) - 1) & (j == pl.num_programs(2) - 1))
    def _():
        o_ref[...] = _rmsnorm(x_ref[...] + acc_sc[...], gf_ref[...])


def _moe(x, gain, w_router_t3, w_gate_up, w_down, gain_final, tm, tf):
    n, d = x.shape
    n_e, d_ff, _ = w_down.shape
    n_f = d_ff // tf
    assert tm % MOE_CHUNK == 0
    mean_rows = tm * 2 // n_e
    c0 = max(MOE_CHUNK, -(-(mean_rows + mean_rows // 8) // BF16_SUBLANES) * BF16_SUBLANES)
    cap = c0 + -(-(tm - c0) // MOE_CHUNK) * MOE_CHUNK
    row_spec = pl.BlockSpec((tm, d), lambda i, e, j: (i, 0))
    vec_spec = pl.BlockSpec((1, d), lambda i, e, j: (0, 0))
    return pl.pallas_call(
        functools.partial(_moe_kernel, c0=c0),
        grid=(n // tm, n_e, n_f),
        in_specs=[row_spec, vec_spec,
                  pl.BlockSpec((N_SPLIT, LANES, d), lambda i, e, j: (0, 0, 0)),
                  pl.BlockSpec((None, d, tf), lambda i, e, j: (e, 0, j)),
                  pl.BlockSpec((None, d, tf), lambda i, e, j: (e, 0, j + n_f)),
                  pl.BlockSpec((None, tf, d), lambda i, e, j: (e, j, 0)),
                  vec_spec],
        out_specs=row_spec,
        out_shape=jax.ShapeDtypeStruct((n, d), F32),
        scratch_shapes=[pltpu.VMEM((tm, d), BF16), pltpu.VMEM((LANES, tm), F32), pltpu.VMEM((LANES, tm), F32),
                        pltpu.VMEM((cap, d), BF16), pltpu.VMEM((cap, 1), F32), pltpu.VMEM((cap, d), F32),
                        pltpu.VMEM((tm, d), F32)],
        compiler_params=_params("parallel", "arbitrary", "arbitrary"),
        name="moe_swiglu_final_norm",
    )(x, gain, w_router_t3, w_gate_up, w_gate_up, w_down, gain_final)


def _row_tile(n, target):
    t = min(n, target)
    while n % t:
        t //= 2
    return t


def kernel(x_prompt, x_sample, cache_k, cache_v, cache_logf, state_pool, page_table, l0_norm_attn, l0_w_qkvf, l0_b_f, l0_w_o, l0_norm_ffn, l0_w_gate_up, l0_w_down, l1_norm_pool, l1_w_group, l1_pool_scale, l1_norm_ffn, l1_w_router, l1_w_gate_up, l1_w_down, final_norm):
    b_p, s_p, d = x_prompt.shape
    b_s, t_new, _ = x_sample.shape
    assert b_p == 1 and d == N_HEADS * HEAD_DIM
    n_phys, page = cache_k.shape[:2]
    past_len = page_table.shape[1] * page
    d_ff = l0_w_down.shape[0]

    row = lambda v: v.reshape(1, -1).astype(F32)
    w_t = l0_w_qkvf.T.astype(BF16)
    wf_pad = jnp.pad(w_t[3 * d:], ((0, LANES - N_HEADS), (0, 0)))
    bf = jnp.pad(l0_b_f, (0, LANES - N_HEADS)).reshape(1, LANES)
    bft = l0_b_f.reshape(N_HEADS, 1)
    wo = l0_w_o.astype(BF16)
    tf = 512 if d_ff % 512 == 0 else d_ff
    w0_gu = l0_w_gate_up.astype(BF16)
    w0_d = l0_w_down.astype(BF16)
    wgrp = l1_w_group.astype(BF16)
    wr_pad = jnp.pad(l1_w_router.T, ((0, LANES - N_EXPERTS), (0, 0)))
    wr_hi = wr_pad.astype(BF16)
    wr_r1 = wr_pad - wr_hi.astype(F32)
    wr_mid = wr_r1.astype(BF16)
    wr_lo = (wr_r1 - wr_mid.astype(F32)).astype(BF16)
    wr3 = jnp.stack([wr_hi, wr_mid, wr_lo])
    w1_gu = l1_w_gate_up.astype(BF16)
    w1_d = l1_w_down.astype(BF16)

    xp = x_prompt.reshape(s_p, d)
    xs = x_sample.reshape(b_s * t_new, d)
    n_s = b_s * t_new

    blk = _row_tile(s_p, 512)
    qt_p, kh_p, kt_p, vt_p, vtb_p, lf_p, lft_p = _qkvf_prompt(xp, row(l0_norm_attn), w_t, wf_pad, bf, bft, blk)
    q_s, k_s, v_s, lf_s = _qkvf_sample(xs, row(l0_norm_attn), w_t, wf_pad, bf, _row_tile(n_s, 512))

    attn_p = _prompt_attn(qt_p, _key_aug(lf_p, kh_p, blk), vtb_p, _row_tile(s_p, ATTN_BLOCK))

    feat_major = lambda c: jnp.transpose(c, (0, 2, 3, 1)).reshape(n_phys, d, page)
    n_pp = math.gcd(page_table.shape[1], PAGES_PER_STEP)
    attn_s = _sample_attn(page_table, feat_major(cache_k), feat_major(cache_v), jnp.swapaxes(cache_logf, 1, 2),
                          q_s, k_s, v_s, lf_s, n_pp)

    hp = _oproj(attn_p, xp, wo, blk, True)
    hs = _oproj(attn_s, xs, wo, _row_tile(n_s, 512), False)

    hp = _ffn(hp, row(l0_norm_ffn), w0_gu, w0_d, _row_tile(s_p, 1024), tf)
    hs = _ffn(hs, row(l0_norm_ffn), w0_gu, w0_d, _row_tile(n_s, 1024), tf)

    hp, tail_p = _pool_prompt(hp, row(l1_norm_pool), wgrp, row(l1_pool_scale), _row_tile(s_p, 512))
    state_pad = jnp.pad(state_pool, ((0, 0), (POOL_PAD - POOL_HIST, 0), (0, 0)))
    hs, hist_s = _pool_sample(hs, state_pad, row(l1_norm_pool), wgrp, row(l1_pool_scale), t_new, past_len)

    yp = _moe(hp, row(l1_norm_ffn), wr3, w1_gu, w1_d, row(final_norm), _row_tile(s_p, 1024), tf)
    ys = _moe(hs, row(l1_norm_ffn), wr3, w1_gu, w1_d, row(final_norm), _row_tile(n_s, 1024), tf)

    heads_t = lambda a: jnp.transpose(a.reshape(1, N_HEADS, HEAD_DIM, s_p), (0, 3, 1, 2))
    heads = lambda a: a.reshape(b_s, t_new, N_HEADS, HEAD_DIM)
    return (yp.reshape(b_p, s_p, d), ys.reshape(b_s, t_new, d),
            heads_t(kt_p), heads_t(vt_p), jnp.transpose(lft_p.reshape(1, N_HEADS, s_p), (0, 2, 1)),
            tail_p[POOL_PAD - POOL_HIST:].reshape(b_p, POOL_HIST, d),
            heads(k_s), heads(v_s), lf_s[:, :N_HEADS].reshape(b_s, t_new, N_HEADS),
            hist_s[:, POOL_PAD - POOL_HIST:, :])
```

```python
import functools
import math

import jax
import jax.numpy as jnp
from jax import lax
from jax.experimental import pallas as pl
from jax.experimental.pallas import tpu as pltpu

N_HEADS = 16
HEAD_DIM = 64
RMS_EPS = 1e-6
POOL_WINDOWS = (2, 4, 8, 16)
POOL_HIST = max(POOL_WINDOWS) - 1
POOL_PAD = POOL_HIST + 1
N_EXPERTS = 8
LANES = 128
BF16_SUBLANES = 16
MXU_WIDTH = 256
N_SPLIT = 3
NEG_BIG = -1e30
LOG2_E = 1.4426950408889634
MOE_CHUNK = 128
ATTN_HEADS = 2
ATTN_BLOCK = 512
PAGES_PER_STEP = 16
VMEM_LIMIT_BYTES = 56 * 1024 * 1024

F32 = jnp.float32
BF16 = jnp.bfloat16
_NT = (((1,), (1,)), ((), ()))
_TN = (((0,), (0,)), ((), ()))


def _params(*sem):
    return pltpu.CompilerParams(dimension_semantics=sem, vmem_limit_bytes=VMEM_LIMIT_BYTES)


def _rmsnorm(x, gain):
    ms = jnp.mean(x * x, axis=-1, keepdims=True)
    return x * lax.rsqrt(ms + RMS_EPS) * gain


def _split3(x):
    hi = x.astype(BF16)
    r1 = x - hi.astype(F32)
    mid = r1.astype(BF16)
    lo = (r1 - mid.astype(F32)).astype(BF16)
    return hi, mid, lo


def _dot(a, b):
    return jnp.dot(a, b, preferred_element_type=F32)


def _dot_nt(a, b):
    return lax.dot_general(a, b, _NT, preferred_element_type=F32)


def _dot_exact_rhs(a_bf16, x_f32):
    hi, mid, lo = _split3(x_f32)
    return (_dot(a_bf16, hi) + _dot(a_bf16, mid)) + _dot(a_bf16, lo)


def _dot_exact_lhs(x_f32, a_bf16):
    hi, mid, lo = _split3(x_f32)
    return (_dot(hi, a_bf16) + _dot(mid, a_bf16)) + _dot(lo, a_bf16)


def _log_sigmoid(z):
    return jnp.minimum(z, 0.0) - jnp.log1p(jnp.exp(-jnp.abs(z)))


def _full_spec(shape):
    return pl.BlockSpec(shape, lambda *_: (0,) * len(shape))


def _qkvf_prompt_kernel(x_ref, g_ref, wq_ref, wk_ref, wv_ref, wf_ref, wfp_ref, bf_ref, bft_ref,
                        qt_ref, kh_ref, kt_ref, vt_ref, vtb_ref, lf_ref, lft_ref):
    xn = _rmsnorm(x_ref[...], g_ref[...]).astype(BF16)
    qt_ref[...] = (_dot_nt(wq_ref[...], xn) * (LOG2_E * HEAD_DIM ** -0.5)).astype(BF16)
    kt_ref[...] = _dot_nt(wk_ref[...], xn)
    vt = _dot_nt(wv_ref[...], xn)
    vt_ref[...] = vt
    vtb_ref[...] = vt.astype(BF16)
    k = _dot_nt(xn, wk_ref[...])
    for h in range(N_HEADS):
        kh_ref[h] = k[:, h * HEAD_DIM:(h + 1) * HEAD_DIM].astype(BF16)
    lf_ref[...] = _log_sigmoid(_dot_nt(xn, wfp_ref[...]) + bf_ref[...])
    lft_ref[...] = _log_sigmoid(_dot_nt(wf_ref[...], xn) + bft_ref[...])


def _qkvf_prompt(x, gain, w_t, wf_pad, bf, bft, tm):
    n, d = x.shape
    feat_spec = pl.BlockSpec((d, tm), lambda i: (0, i))
    w_spec = lambda blk: pl.BlockSpec((d, d), lambda i: (blk, 0))
    feat = lambda dt: jax.ShapeDtypeStruct((d, n), dt)
    return pl.pallas_call(
        _qkvf_prompt_kernel,
        grid=(n // tm,),
        in_specs=[pl.BlockSpec((tm, d), lambda i: (i, 0)), _full_spec((1, d)),
                  w_spec(0), w_spec(1), w_spec(2),
                  pl.BlockSpec((N_HEADS, d), lambda i: (3 * d // N_HEADS, 0)),
                  _full_spec((LANES, d)), _full_spec((1, LANES)), _full_spec((N_HEADS, 1))],
        out_specs=[feat_spec, pl.BlockSpec((N_HEADS, tm, HEAD_DIM), lambda i: (0, i, 0)),
                   feat_spec, feat_spec, feat_spec,
                   pl.BlockSpec((tm, LANES), lambda i: (i, 0)),
                   pl.BlockSpec((N_HEADS, tm), lambda i: (0, i))],
        out_shape=[feat(BF16), jax.ShapeDtypeStruct((N_HEADS, n, HEAD_DIM), BF16),
                   feat(F32), feat(F32), feat(BF16),
                   jax.ShapeDtypeStruct((n, LANES), F32), jax.ShapeDtypeStruct((N_HEADS, n), F32)],
        compiler_params=_params("parallel"),
        name="qkvf_proj_prompt",
    )(x, gain, w_t, w_t, w_t, w_t, wf_pad, bf, bft)


def _qkvf_sample_kernel(x_ref, g_ref, wq_ref, wk_ref, wv_ref, wfp_ref, bf_ref, q_ref, k_ref, v_ref, lf_ref):
    xn = _rmsnorm(x_ref[...], g_ref[...]).astype(BF16)
    q_ref[...] = _dot_nt(xn, wq_ref[...]) * (HEAD_DIM ** -0.5)
    k_ref[...] = _dot_nt(xn, wk_ref[...])
    v_ref[...] = _dot_nt(xn, wv_ref[...])
    lf_ref[...] = _log_sigmoid(_dot_nt(xn, wfp_ref[...]) + bf_ref[...])


def _qkvf_sample(x, gain, w_t, wf_pad, bf, tm):
    n, d = x.shape
    row_spec = pl.BlockSpec((tm, d), lambda i: (i, 0))
    w_spec = lambda blk: pl.BlockSpec((d, d), lambda i: (blk, 0))
    rows = jax.ShapeDtypeStruct((n, d), F32)
    return pl.pallas_call(
        _qkvf_sample_kernel,
        grid=(n // tm,),
        in_specs=[row_spec, _full_spec((1, d)), w_spec(0), w_spec(1), w_spec(2),
                  _full_spec((LANES, d)), _full_spec((1, LANES))],
        out_specs=[row_spec, row_spec, row_spec, pl.BlockSpec((tm, LANES), lambda i: (i, 0))],
        out_shape=[rows, rows, rows, jax.ShapeDtypeStruct((n, LANES), F32)],
        compiler_params=_params("parallel"),
        name="qkvf_proj_sample",
    )(x, gain, w_t, w_t, w_t, wf_pad, bf)


def _key_aug_kernel(lf_ref, kh_ref, ka_ref, c_ref):
    @pl.when(pl.program_id(0) == 0)
    def _():
        c_ref[...] = jnp.zeros_like(c_ref)

    t = lf_ref.shape[0]
    row = lax.broadcasted_iota(jnp.int32, (t, t), 0)
    col = lax.broadcasted_iota(jnp.int32, (t, t), 1)
    fc = _dot_exact_rhs((col <= row).astype(BF16), lf_ref[...]) + c_ref[...]
    c_ref[...] = fc[t - 1:t, :]
    lane = lax.broadcasted_iota(jnp.int32, (t, HEAD_DIM), 1)
    for h in range(N_HEADS):
        terms = _split3(jnp.broadcast_to(-LOG2_E * fc[:, h:h + 1], (t, HEAD_DIM)))
        aug = jnp.zeros((t, HEAD_DIM), F32)
        for i in range(N_SPLIT):
            aug = jnp.where(lane == i, terms[i].astype(F32), aug)
        ka_ref[h] = jnp.concatenate([kh_ref[h], aug.astype(BF16)], axis=1)


def _key_aug(lf, kh, t):
    s = lf.shape[0]
    return pl.pallas_call(
        _key_aug_kernel,
        grid=(s // t,),
        in_specs=[pl.BlockSpec((t, LANES), lambda i: (i, 0)),
                  pl.BlockSpec((N_HEADS, t, HEAD_DIM), lambda i: (0, i, 0))],
        out_specs=pl.BlockSpec((N_HEADS, t, 2 * HEAD_DIM), lambda i: (0, i, 0)),
        out_shape=jax.ShapeDtypeStruct((N_HEADS, s, 2 * HEAD_DIM), BF16),
        scratch_shapes=[pltpu.VMEM((1, LANES), F32)],
        compiler_params=_params("arbitrary"),
        name="key_aug_cumsum",
    )(lf, kh)


def _prompt_attn_kernel(qt_ref, ka_ref, vt_ref, o_ref, s_sc, mc_sc, m_sc, acc_sc, *, blk):
    qi = pl.program_id(1)
    ones_rows = (lax.broadcasted_iota(jnp.int32, (HEAD_DIM, blk), 0) < N_SPLIT).astype(BF16)
    qa = [jnp.concatenate([qt_ref[pl.ds(hh * HEAD_DIM, HEAD_DIM), :], ones_rows], axis=0)
          for hh in range(ATTN_HEADS)]
    m_sc[...] = jnp.full_like(m_sc, NEG_BIG)
    acc_sc[...] = jnp.zeros_like(acc_sc)
    kidx = lax.broadcasted_iota(jnp.int32, (blk, blk), 0)
    qidx = lax.broadcasted_iota(jnp.int32, (blk, blk), 1)
    ones_blk = jnp.ones((BF16_SUBLANES, blk), BF16)

    def scores(ki, slot, causal):
        ks = pl.multiple_of(ki * blk, blk)
        for hh in range(ATTN_HEADS):
            s = _dot(ka_ref[hh, pl.ds(ks, blk), :], qa[hh])
            if causal:
                s = jnp.where(kidx <= qidx, s, NEG_BIG)
            s_sc[hh, slot] = s
            mc_sc[hh, slot] = jnp.max(s, axis=0, keepdims=True)

    def softmax_pv(ki, slot):
        ks = pl.multiple_of(ki * blk, blk)
        for hh in range(ATTN_HEADS):
            m_prev = m_sc[hh]
            m_new = jnp.maximum(m_prev, mc_sc[hh, slot])
            alpha = jnp.exp2(m_prev - m_new)
            p = jnp.exp2(s_sc[hh, slot] - m_new).astype(BF16)
            v_blk = jnp.concatenate([vt_ref[pl.ds(hh * HEAD_DIM, HEAD_DIM), pl.ds(ks, blk)], ones_blk], axis=0)
            acc_sc[hh] = alpha * acc_sc[hh] + _dot(v_blk, p)
            m_sc[hh] = m_new

    @pl.when(qi == 0)
    def _():
        scores(0, 0, True)
        softmax_pv(0, 0)

    @pl.when(qi > 0)
    def _():
        scores(0, 0, False)
        n_pairs = (qi - 1) // 2

        def body(k, carry):
            scores(2 * k + 1, 1, False)
            softmax_pv(2 * k, 0)
            scores(2 * k + 2, 0, False)
            softmax_pv(2 * k + 1, 1)
            return carry

        lax.fori_loop(0, n_pairs, body, 0)
        i0 = 2 * n_pairs

        @pl.when(qi - i0 == 2)
        def _():
            scores(i0 + 1, 1, False)
            softmax_pv(i0, 0)
            scores(qi, 0, True)
            softmax_pv(i0 + 1, 1)
            softmax_pv(qi, 0)

        @pl.when(qi - i0 == 1)
        def _():
            scores(qi, 1, True)
            softmax_pv(i0, 0)
            softmax_pv(qi, 1)

    for hh in range(ATTN_HEADS):
        acc = acc_sc[hh]
        o_ref[pl.ds(hh * HEAD_DIM, HEAD_DIM), :] = (acc[:HEAD_DIM] / acc[HEAD_DIM:HEAD_DIM + 1]).astype(o_ref.dtype)


def _prompt_attn(qt, ka, vt, blk):
    d, s = qt.shape
    rows = ATTN_HEADS * HEAD_DIM
    return pl.pallas_call(
        functools.partial(_prompt_attn_kernel, blk=blk),
        grid=(N_HEADS // ATTN_HEADS, s // blk),
        in_specs=[pl.BlockSpec((rows, blk), lambda h, i: (h, i)),
                  pl.BlockSpec((ATTN_HEADS, s, 2 * HEAD_DIM), lambda h, i: (h, 0, 0)),
                  pl.BlockSpec((rows, s), lambda h, i: (h, 0))],
        out_specs=pl.BlockSpec((rows, blk), lambda h, i: (h, i)),
        out_shape=jax.ShapeDtypeStruct((d, s), BF16),
        scratch_shapes=[pltpu.VMEM((ATTN_HEADS, 2, blk, blk), F32), pltpu.VMEM((ATTN_HEADS, 2, 1, blk), F32),
                        pltpu.VMEM((ATTN_HEADS, 1, blk), F32),
                        pltpu.VMEM((ATTN_HEADS, HEAD_DIM + BF16_SUBLANES, blk), F32)],
        compiler_params=_params("parallel", "parallel"),
        name="fox_prompt_attn",
    )(qt, ka, vt)


def _sample_attn_kernel(pt_ref, *refs, n_pp, page, t_new):
    del pt_ref
    k_refs = refs[:n_pp]
    v_refs = refs[n_pp:2 * n_pp]
    lf_refs = refs[2 * n_pp:3 * n_pp]
    q_ref, kn_ref, vn_ref, lfn_ref, o_ref, qbd_sc, m_sc, l_sc, acc_sc, suf_sc = refs[3 * n_pp:]
    j = pl.program_id(1)
    rows = N_HEADS * t_new
    d = N_HEADS * HEAD_DIM
    own = (lax.broadcasted_iota(jnp.int32, (rows, d), 1) // HEAD_DIM
           == lax.broadcasted_iota(jnp.int32, (rows, d), 0) // t_new)

    def online_update(s, pv_fn):
        m_prev = m_sc[...]
        m_new = jnp.maximum(m_prev, jnp.max(s, axis=1, keepdims=True))
        alpha = jnp.exp(m_prev - m_new)
        p = jnp.exp(s - m_new)
        l_sc[...] = alpha * l_sc[...] + jnp.sum(p, axis=1, keepdims=True)
        acc_sc[...] = alpha * acc_sc[...] + pv_fn(p.astype(BF16))
        m_sc[...] = m_new

    def add_head_rows(s, bias):
        return jnp.concatenate(
            [s[h * t_new:(h + 1) * t_new, :] + bias[h:h + 1, :] for h in range(N_HEADS)], axis=0)

    @pl.when(j == 0)
    def _():
        m_sc[...] = jnp.full_like(m_sc, NEG_BIG)
        l_sc[...] = jnp.zeros_like(l_sc)
        acc_sc[...] = jnp.zeros_like(acc_sc)
        suf_sc[...] = jnp.zeros_like(suf_sc)
        q_rep = jnp.concatenate([q_ref[...]] * N_HEADS, axis=0)
        qbd_sc[...] = jnp.where(own, q_rep, 0.0).astype(BF16)
        r8 = lax.broadcasted_iota(jnp.int32, (t_new, t_new), 0)
        c8 = lax.broadcasted_iota(jnp.int32, (t_new, t_new), 1)
        cum = _dot_exact_rhs((c8 <= r8).astype(BF16), lfn_ref[...])
        cum_pad = jnp.concatenate([cum, jnp.zeros((LANES - t_new, LANES), F32)], axis=0)
        eye = (lax.broadcasted_iota(jnp.int32, (N_HEADS, LANES), 0)
               == lax.broadcasted_iota(jnp.int32, (N_HEADS, LANES), 1)).astype(BF16)
        hi, mid, lo = _split3(cum_pad)
        cum_t = (_dot_nt(eye, hi) + _dot_nt(eye, mid)) + _dot_nt(eye, lo)
        zpad = jnp.zeros((LANES - t_new, d), F32)
        kn = jnp.concatenate([kn_ref[...], zpad], axis=0).astype(BF16)
        vn = jnp.concatenate([vn_ref[...], zpad], axis=0).astype(BF16)
        s = add_head_rows(_dot_nt(qbd_sc[...], kn), -cum_t)
        qt = lax.broadcasted_iota(jnp.int32, (rows, LANES), 0) % t_new
        kt = lax.broadcasted_iota(jnp.int32, (rows, LANES), 1)
        s = jnp.where(kt <= qt, s, NEG_BIG)
        online_update(s, lambda p: _dot(p, vn))

    rp = lax.broadcasted_iota(jnp.int32, (page, page), 0)
    cp = lax.broadcasted_iota(jnp.int32, (page, page), 1)
    newer = (rp > cp).astype(BF16)
    carry = suf_sc[...]
    suf_pages = [None] * n_pp
    for i in reversed(range(n_pp)):
        lf = lf_refs[i][...]
        suf_pages[i] = _dot_exact_lhs(lf, newer) + carry
        carry = carry + jnp.sum(lf, axis=1, keepdims=True)
    suf_sc[...] = carry
    suf = jnp.concatenate(suf_pages, axis=1)

    kt_pages = jnp.concatenate([r[...] for r in k_refs], axis=1).astype(BF16)
    vt_pages = jnp.concatenate([r[...] for r in v_refs], axis=1).astype(BF16)
    s = add_head_rows(_dot(qbd_sc[...], kt_pages), suf)
    online_update(s, lambda p: _dot_nt(p, vt_pages))

    @pl.when(j == pl.num_programs(1) - 1)
    def _():
        out = jnp.where(own, acc_sc[...] / l_sc[...], 0.0)
        o = out[0:t_new, :]
        for h in range(1, N_HEADS):
            o = o + out[h * t_new:(h + 1) * t_new, :]
        o_ref[...] = o


def _sample_attn(page_table, cache_kt, cache_vt, cache_lft, q_s, k_s, v_s, lf_s, n_pp):
    n_seq, n_pages = page_table.shape
    _, d, page = cache_kt.shape
    t_new = q_s.shape[0] // n_seq
    n_grp = n_pages // n_pp

    def page_idx(i):
        return lambda b, j, pt: (pt[b, (n_grp - 1 - j) * n_pp + i], 0, 0)

    kv_specs = [pl.BlockSpec((None, d, page), page_idx(i)) for i in range(n_pp)]
    lf_specs = [pl.BlockSpec((None, N_HEADS, page), page_idx(i)) for i in range(n_pp)]
    tok_spec = pl.BlockSpec((t_new, d), lambda b, j, pt: (b, 0))
    rows = N_HEADS * t_new
    grid_spec = pltpu.PrefetchScalarGridSpec(
        num_scalar_prefetch=1,
        grid=(n_seq, n_grp),
        in_specs=kv_specs + kv_specs + lf_specs + [
            tok_spec, tok_spec, tok_spec, pl.BlockSpec((t_new, LANES), lambda b, j, pt: (b, 0))],
        out_specs=tok_spec,
        scratch_shapes=[pltpu.VMEM((rows, d), BF16), pltpu.VMEM((rows, 1), F32), pltpu.VMEM((rows, 1), F32),
                        pltpu.VMEM((rows, d), F32), pltpu.VMEM((N_HEADS, 1), F32)],
    )
    return pl.pallas_call(
        functools.partial(_sample_attn_kernel, n_pp=n_pp, page=page, t_new=t_new),
        grid_spec=grid_spec,
        out_shape=jax.ShapeDtypeStruct((n_seq * t_new, d), F32),
        compiler_params=_params("parallel", "arbitrary"),
        name="fox_sample_attn",
    )(page_table, *([cache_kt] * n_pp), *([cache_vt] * n_pp), *([cache_lft] * n_pp), q_s, k_s, v_s, lf_s)


def _oproj_kernel(a_ref, x_ref, wo_ref, o_ref, *, feature_major):
    a = a_ref[...].astype(BF16)
    if feature_major:
        proj = lax.dot_general(a, wo_ref[...], _TN, preferred_element_type=F32)
    else:
        proj = _dot(a, wo_ref[...])
    o_ref[...] = x_ref[...] + proj


def _oproj(attn, x, wo, tm, feature_major):
    n, d = x.shape
    row_spec = pl.BlockSpec((tm, d), lambda i: (i, 0))
    attn_spec = pl.BlockSpec((d, tm), lambda i: (0, i)) if feature_major else row_spec
    return pl.pallas_call(
        functools.partial(_oproj_kernel, feature_major=feature_major),
        grid=(n // tm,),
        in_specs=[attn_spec, row_spec, _full_spec((d, d))],
        out_specs=row_spec,
        out_shape=jax.ShapeDtypeStruct((n, d), F32),
        compiler_params=_params("parallel"),
        name="attn_out_proj",
    )(attn, x, wo)


def _swiglu_act(x_bf16, wg_ref, wu_ref):
    gate = _dot(x_bf16, wg_ref[...])
    return gate * jax.nn.sigmoid(gate) * _dot(x_bf16, wu_ref[...])


def _ffn_kernel(x_ref, g_ref, wg_ref, wu_ref, wd_ref, o_ref, xn_sc, acc_sc):
    j = pl.program_id(1)

    @pl.when(j == 0)
    def _():
        xn_sc[...] = _rmsnorm(x_ref[...], g_ref[...]).astype(BF16)
        acc_sc[...] = jnp.zeros_like(acc_sc)

    act = _swiglu_act(xn_sc[...], wg_ref, wu_ref).astype(BF16)
    acc_sc[...] += _dot(act, wd_ref[...])

    @pl.when(j == pl.num_programs(1) - 1)
    def _():
        o_ref[...] = x_ref[...] + acc_sc[...]


def _ffn(x, gain, w_gate_up, w_down, tm, tf):
    n, d = x.shape
    n_f = w_down.shape[0] // tf
    row_spec = pl.BlockSpec((tm, d), lambda i, j: (i, 0))
    return pl.pallas_call(
        _ffn_kernel,
        grid=(n // tm, n_f),
        in_specs=[row_spec, pl.BlockSpec((1, d), lambda i, j: (0, 0)),
                  pl.BlockSpec((d, tf), lambda i, j: (0, j)),
                  pl.BlockSpec((d, tf), lambda i, j: (0, j + n_f)),
                  pl.BlockSpec((tf, d), lambda i, j: (j, 0))],
        out_specs=row_spec,
        out_shape=jax.ShapeDtypeStruct((n, d), F32),
        scratch_shapes=[pltpu.VMEM((tm, d), BF16), pltpu.VMEM((tm, d), F32)],
        compiler_params=_params("parallel", "arbitrary"),
        name="dense_swiglu",
    )(x, gain, w_gate_up, w_gate_up, w_down)


def _pool_mix(ext_ref, x, u, pos0, wgrp_ref, scale_ref, o_ref, n_rows):
    gdim = u.shape[1] // len(POOL_WINDOWS)
    pos = pos0 + lax.broadcasted_iota(jnp.int32, (n_rows, 1), 0)
    for g, w in enumerate(POOL_WINDOWS):
        cols = pl.ds(g * gdim, gdim)
        win = ext_ref[pl.ds(POOL_PAD, n_rows), cols]
        for i in range(1, w):
            win = win + ext_ref[pl.ds(POOL_PAD - i, n_rows), cols]
        count = jnp.minimum(pos + 1, w).astype(F32)
        pooled = (win / count - u[:, g * gdim:(g + 1) * gdim]).astype(BF16)
        mixed = _dot(pooled, wgrp_ref[g]) * scale_ref[:, cols]
        o_ref[:, cols] = x[:, g * gdim:(g + 1) * gdim] + mixed


def _pool_prompt_kernel(x_ref, g_ref, wgrp_ref, scale_ref, o_ref, tail_ref, ext_sc, *, tm):
    i = pl.program_id(0)

    @pl.when(i == 0)
    def _():
        ext_sc[pl.ds(0, POOL_PAD), :] = jnp.zeros((POOL_PAD, ext_sc.shape[1]), F32)

    @pl.when(i > 0)
    def _():
        ext_sc[pl.ds(0, POOL_PAD), :] = ext_sc[pl.ds(tm, POOL_PAD), :]

    x = x_ref[...]
    u = _rmsnorm(x, g_ref[...])
    ext_sc[pl.ds(POOL_PAD, tm), :] = u
    tail_ref[...] = u[tm - POOL_PAD:, :]
    _pool_mix(ext_sc, x, u, i * tm, wgrp_ref, scale_ref, o_ref, tm)


def _pool_prompt(x, gain, w_group, scale, tm):
    n, d = x.shape
    gdim = d // len(POOL_WINDOWS)
    row_spec = pl.BlockSpec((tm, d), lambda i: (i, 0))
    return pl.pallas_call(
        functools.partial(_pool_prompt_kernel, tm=tm),
        grid=(n // tm,),
        in_specs=[row_spec, pl.BlockSpec((1, d), lambda i: (0, 0)),
                  pl.BlockSpec((len(POOL_WINDOWS), gdim, gdim), lambda i: (0, 0, 0)),
                  pl.BlockSpec((1, d), lambda i: (0, 0))],
        out_specs=[row_spec, pl.BlockSpec((POOL_PAD, d), lambda i: (0, 0))],
        out_shape=[jax.ShapeDtypeStruct((n, d), F32), jax.ShapeDtypeStruct((POOL_PAD, d), F32)],
        scratch_shapes=[pltpu.VMEM((POOL_PAD + tm, d), F32)],
        compiler_params=_params("arbitrary"),
        name="pool_mix_prompt",
    )(x, gain, w_group, scale)


def _pool_sample_kernel(x_ref, st_ref, g_ref, wgrp_ref, scale_ref, o_ref, hist_ref, ext_sc, *, t_new, past_len):
    x = x_ref[...]
    u = _rmsnorm(x, g_ref[...])
    ext_sc[pl.ds(0, POOL_PAD), :] = st_ref[...]
    ext_sc[pl.ds(POOL_PAD, t_new), :] = u
    hist_ref[...] = ext_sc[pl.ds(t_new, POOL_PAD), :]
    _pool_mix(ext_sc, x, u, past_len, wgrp_ref, scale_ref, o_ref, t_new)


def _pool_sample(x, state_pad, gain, w_group, scale, t_new, past_len):
    n, d = x.shape
    gdim = d // len(POOL_WINDOWS)
    row_spec = pl.BlockSpec((t_new, d), lambda b: (b, 0))
    st_spec = pl.BlockSpec((None, POOL_PAD, d), lambda b: (b, 0, 0))
    return pl.pallas_call(
        functools.partial(_pool_sample_kernel, t_new=t_new, past_len=past_len),
        grid=(n // t_new,),
        in_specs=[row_spec, st_spec, pl.BlockSpec((1, d), lambda b: (0, 0)),
                  pl.BlockSpec((len(POOL_WINDOWS), gdim, gdim), lambda b: (0, 0, 0)),
                  pl.BlockSpec((1, d), lambda b: (0, 0))],
        out_specs=[row_spec, st_spec],
        out_shape=[jax.ShapeDtypeStruct((n, d), F32), jax.ShapeDtypeStruct(state_pad.shape, F32)],
        scratch_shapes=[pltpu.VMEM((POOL_PAD + t_new, d), F32)],
        compiler_params=_params("parallel"),
        name="pool_mix_sample",
    )(x, state_pad, gain, w_group, scale)


def _moe_kernel(x_ref, g_ref, wrt_ref, wg_ref, wu_ref, wd_ref, gf_ref, o_ref,
                xn_sc, gate_sc, rank_sc, xe_sc, ge_sc, ye_sc, *, c0):
    e = pl.program_id(1)
    j = pl.program_id(2)
    tm = x_ref.shape[0]
    e_rows = gate_sc.shape[0]

    @pl.when((e == 0) & (j == 0))
    def _():
        xn = _rmsnorm(x_ref[...], g_ref[...])
        xh, xm, xl = _split3(xn)
        xn_sc[...] = xh
        wh, wm, wl = wrt_ref[0], wrt_ref[1], wrt_ref[2]
        logits = ((_dot_nt(wl, xh) + _dot_nt(wm, xm) + _dot_nt(wh, xl))
                  + (_dot_nt(wm, xh) + _dot_nt(wh, xm))) + _dot_nt(wh, xh)
        erow = lax.broadcasted_iota(jnp.int32, (e_rows, tm), 0)
        logits = jnp.where(erow < N_EXPERTS, logits, -jnp.inf)
        top1 = jnp.max(logits, axis=0, keepdims=True)
        idx1 = jnp.min(jnp.where(logits == top1, erow, e_rows), axis=0, keepdims=True)
        rest = jnp.where(erow == idx1, -jnp.inf, logits)
        top2 = jnp.max(rest, axis=0, keepdims=True)
        idx2 = jnp.min(jnp.where(rest == top2, erow, e_rows), axis=0, keepdims=True)
        e2 = jnp.exp(top2 - top1)
        denom = 1.0 + e2
        gate_sc[...] = jnp.where(erow == idx1, 1.0 / denom, 0.0) + jnp.where(erow == idx2, e2 / denom, 0.0)
        routed = (erow == idx1) | (erow == idx2)
        earlier = jnp.where(lax.broadcasted_iota(jnp.int32, (tm, tm), 0)
                            < lax.broadcasted_iota(jnp.int32, (tm, tm), 1), 1.0, 0.0).astype(BF16)
        rank = _dot(jnp.where(routed, 1.0, 0.0).astype(BF16), earlier)
        rank_sc[...] = jnp.where(routed, rank, -1.0)
        o_ref[...] = x_ref[...]

    rank_e = rank_sc[pl.ds(e, 1), :]
    count = jnp.sum(jnp.where(rank_e >= 0.0, 1.0, 0.0)).astype(jnp.int32)
    n_extra = lax.shift_right_logical(jnp.maximum(count - c0, 0) + (MOE_CHUNK - 1), MOE_CHUNK.bit_length() - 1)

    def for_chunks(fn):
        fn(0, c0)

        def body(k, carry):
            fn(pl.multiple_of(c0 + k * MOE_CHUNK, BF16_SUBLANES), MOE_CHUNK)
            return carry

        lax.fori_loop(0, n_extra, body, 0)

    def selected(base, rows, value):
        slot = (base + lax.broadcasted_iota(jnp.int32, (rows, 1), 0)).astype(F32)
        return jnp.where(rank_e == slot, value, 0.0)

    def selection(base, rows):
        return selected(base, rows, 1.0).astype(BF16)

    def gather(base, rows):
        xe_sc[pl.ds(base, rows), :] = _dot(selection(base, rows), xn_sc[...]).astype(BF16)
        gate_e = gate_sc[pl.ds(e, 1), :]
        ge_sc[pl.ds(base, rows), :] = jnp.sum(selected(base, rows, gate_e), axis=1, keepdims=True)
        ye_sc[pl.ds(base, rows), :] = jnp.zeros((rows, ye_sc.shape[1]), F32)

    def expert(base, rows):
        act = _swiglu_act(xe_sc[pl.ds(base, rows), :], wg_ref, wu_ref) * ge_sc[pl.ds(base, rows), :]
        ye_sc[pl.ds(base, rows), :] += _dot(act.astype(BF16), wd_ref[...])

    def scatter(base, rows):
        y = ye_sc[pl.ds(base, rows), :].astype(BF16)
        o_ref[...] += lax.dot_general(selection(base, rows), y, _TN, preferred_element_type=F32)

    @pl.when(j == 0)
    def _():
        for_chunks(gather)

    for_chunks(expert)

    @pl.when(j == pl.num_programs(2) - 1)
    def _():
        for_chunks(scatter)

    @pl.when((e == pl.num_programs(1) - 1) & (j == pl.num_programs(2) - 1))
    def _():
        o_ref[...] = _rmsnorm(o_ref[...], gf_ref[...])


def _moe(x, gain, w_router_t3, w_gate_up, w_down, gain_final, tm, tf):
    n, d = x.shape
    n_e, d_ff, _ = w_down.shape
    n_f = d_ff // tf
    assert tm % MOE_CHUNK == 0
    mean_rows = tm * 2 // n_e
    c0 = max(MOE_CHUNK, -(-(mean_rows + mean_rows // 8) // BF16_SUBLANES) * BF16_SUBLANES)
    cap = c0 + -(-(tm - c0) // MOE_CHUNK) * MOE_CHUNK
    row_spec = pl.BlockSpec((tm, d), lambda i, e, j: (i, 0), pipeline_mode=pl.Buffered(1))
    vec_spec = pl.BlockSpec((1, d), lambda i, e, j: (0, 0))
    return pl.pallas_call(
        functools.partial(_moe_kernel, c0=c0),
        grid=(n // tm, n_e, n_f),
        in_specs=[row_spec, vec_spec,
                  pl.BlockSpec((N_SPLIT, LANES, d), lambda i, e, j: (0, 0, 0)),
                  pl.BlockSpec((None, d, tf), lambda i, e, j: (e, 0, j)),
                  pl.BlockSpec((None, d, tf), lambda i, e, j: (e, 0, j + n_f)),
                  pl.BlockSpec((None, tf, d), lambda i, e, j: (e, j, 0)),
                  vec_spec],
        out_specs=row_spec,
        out_shape=jax.ShapeDtypeStruct((n, d), F32),
        scratch_shapes=[pltpu.VMEM((tm, d), BF16), pltpu.VMEM((LANES, tm), F32), pltpu.VMEM((LANES, tm), F32),
                        pltpu.VMEM((cap, d), BF16), pltpu.VMEM((cap, 1), F32), pltpu.VMEM((cap, d), F32)],
        compiler_params=_params("parallel", "arbitrary", "arbitrary"),
        name="moe_swiglu_final_norm",
    )(x, gain, w_router_t3, w_gate_up, w_gate_up, w_down, gain_final)


def _row_tile(n, target):
    t = min(n, target)
    while n % t:
        t //= 2
    return t


def kernel(x_prompt, x_sample, cache_k, cache_v, cache_logf, state_pool, page_table, l0_norm_attn, l0_w_qkvf, l0_b_f, l0_w_o, l0_norm_ffn, l0_w_gate_up, l0_w_down, l1_norm_pool, l1_w_group, l1_pool_scale, l1_norm_ffn, l1_w_router, l1_w_gate_up, l1_w_down, final_norm):
    b_p, s_p, d = x_prompt.shape
    b_s, t_new, _ = x_sample.shape
    assert b_p == 1 and d == N_HEADS * HEAD_DIM
    n_phys, page = cache_k.shape[:2]
    past_len = page_table.shape[1] * page
    d_ff = l0_w_down.shape[0]

    row = lambda v: v.reshape(1, -1).astype(F32)
    w_t = l0_w_qkvf.T.astype(BF16)
    wf_pad = jnp.pad(w_t[3 * d:], ((0, LANES - N_HEADS), (0, 0)))
    bf = jnp.pad(l0_b_f, (0, LANES - N_HEADS)).reshape(1, LANES)
    bft = l0_b_f.reshape(N_HEADS, 1)
    wo = l0_w_o.astype(BF16)
    tf = 512 if d_ff % 512 == 0 else d_ff
    tf_moe = d_ff // 2 if d_ff % (2 * MXU_WIDTH) == 0 else tf
    w0_gu = l0_w_gate_up.astype(BF16)
    w0_d = l0_w_down.astype(BF16)
    wgrp = l1_w_group.astype(BF16)
    wr_pad = jnp.pad(l1_w_router.T, ((0, LANES - N_EXPERTS), (0, 0)))
    wr_hi = wr_pad.astype(BF16)
    wr_r1 = wr_pad - wr_hi.astype(F32)
    wr_mid = wr_r1.astype(BF16)
    wr_lo = (wr_r1 - wr_mid.astype(F32)).astype(BF16)
    wr3 = jnp.stack([wr_hi, wr_mid, wr_lo])
    w1_gu = l1_w_gate_up.astype(BF16)
    w1_d = l1_w_down.astype(BF16)

    xp = x_prompt.reshape(s_p, d)
    xs = x_sample.reshape(b_s * t_new, d)
    n_s = b_s * t_new

    blk = _row_tile(s_p, 512)
    qt_p, kh_p, kt_p, vt_p, vtb_p, lf_p, lft_p = _qkvf_prompt(xp, row(l0_norm_attn), w_t, wf_pad, bf, bft, blk)
    q_s, k_s, v_s, lf_s = _qkvf_sample(xs, row(l0_norm_attn), w_t, wf_pad, bf, _row_tile(n_s, 512))

    attn_p = _prompt_attn(qt_p, _key_aug(lf_p, kh_p, blk), vtb_p, _row_tile(s_p, ATTN_BLOCK))

    feat_major = lambda c: jnp.transpose(c, (0, 2, 3, 1)).reshape(n_phys, d, page)
    n_pp = math.gcd(page_table.shape[1], PAGES_PER_STEP)
    attn_s = _sample_attn(page_table, feat_major(cache_k), feat_major(cache_v), jnp.swapaxes(cache_logf, 1, 2),
                          q_s, k_s, v_s, lf_s, n_pp)

    hp = _oproj(attn_p, xp, wo, blk, True)
    hs = _oproj(attn_s, xs, wo, _row_tile(n_s, 512), False)

    hp = _ffn(hp, row(l0_norm_ffn), w0_gu, w0_d, _row_tile(s_p, 1024), tf)
    hs = _ffn(hs, row(l0_norm_ffn), w0_gu, w0_d, _row_tile(n_s, 1024), tf)

    hp, tail_p = _pool_prompt(hp, row(l1_norm_pool), wgrp, row(l1_pool_scale), _row_tile(s_p, 512))
    state_pad = jnp.pad(state_pool, ((0, 0), (POOL_PAD - POOL_HIST, 0), (0, 0)))
    hs, hist_s = _pool_sample(hs, state_pad, row(l1_norm_pool), wgrp, row(l1_pool_scale), t_new, past_len)

    yp = _moe(hp, row(l1_norm_ffn), wr3, w1_gu, w1_d, row(final_norm), _row_tile(s_p, 1024), tf_moe)
    ys = _moe(hs, row(l1_norm_ffn), wr3, w1_gu, w1_d, row(final_norm), _row_tile(n_s, 1024), tf_moe)

    heads_t = lambda a: jnp.transpose(a.reshape(1, N_HEADS, HEAD_DIM, s_p), (0, 3, 1, 2))
    heads = lambda a: a.reshape(b_s, t_new, N_HEADS, HEAD_DIM)
    return (yp.reshape(b_p, s_p, d), ys.reshape(b_s, t_new, d),
            heads_t(kt_p), heads_t(vt_p), jnp.transpose(lft_p.reshape(1, N_HEADS, s_p), (0, 2, 1)),
            tail_p[POOL_PAD - POOL_HIST:].reshape(b_p, POOL_HIST, d),
            heads(k_s), heads(v_s), lf_s[:, :N_HEADS].reshape(b_s, t_new, N_HEADS),
            hist_s[:, POOL_PAD - POOL_HIST:, :])
```

```python
import functools
import math

import jax
import jax.numpy as jnp
from jax import lax
from jax.experimental import pallas as pl
from jax.experimental.pallas import tpu as pltpu

N_HEADS = 16
HEAD_DIM = 64
RMS_EPS = 1e-6
POOL_WINDOWS = (2, 4, 8, 16)
POOL_HIST = max(POOL_WINDOWS) - 1
POOL_PAD = POOL_HIST + 1
N_EXPERTS = 8
LANES = 128
BF16_SUBLANES = 16
MXU_WIDTH = 256
N_SPLIT = 3
NEG_BIG = -1e30
LOG2_E = 1.4426950408889634
MOE_CHUNK = 128
ATTN_HEADS = 2
ATTN_BLOCK = 512
PAGES_PER_STEP = 16
VMEM_LIMIT_BYTES = 56 * 1024 * 1024

F32 = jnp.float32
BF16 = jnp.bfloat16
_NT = (((1,), (1,)), ((), ()))
_TN = (((0,), (0,)), ((), ()))


def _params(*sem):
    return pltpu.CompilerParams(dimension_semantics=sem, vmem_limit_bytes=VMEM_LIMIT_BYTES)


def _rmsnorm(x, gain):
    ms = jnp.mean(x * x, axis=-1, keepdims=True)
    return x * lax.rsqrt(ms + RMS_EPS) * gain


def _split3(x):
    hi = x.astype(BF16)
    r1 = x - hi.astype(F32)
    mid = r1.astype(BF16)
    lo = (r1 - mid.astype(F32)).astype(BF16)
    return hi, mid, lo


def _dot(a, b):
    return jnp.dot(a, b, preferred_element_type=F32)


def _dot_nt(a, b):
    return lax.dot_general(a, b, _NT, preferred_element_type=F32)


def _dot_exact_rhs(a_bf16, x_f32):
    hi, mid, lo = _split3(x_f32)
    return (_dot(a_bf16, hi) + _dot(a_bf16, mid)) + _dot(a_bf16, lo)


def _dot_exact_lhs(x_f32, a_bf16):
    hi, mid, lo = _split3(x_f32)
    return (_dot(hi, a_bf16) + _dot(mid, a_bf16)) + _dot(lo, a_bf16)


def _log_sigmoid(z):
    return jnp.minimum(z, 0.0) - jnp.log1p(jnp.exp(-jnp.abs(z)))


def _full_spec(shape):
    return pl.BlockSpec(shape, lambda *_: (0,) * len(shape))


def _qkvf_prompt_kernel(x_ref, g_ref, wq_ref, wk_ref, wv_ref, wf_ref, wfp_ref, bf_ref, bft_ref,
                        qt_ref, kh_ref, kt_ref, vt_ref, vtb_ref, lf_ref, lft_ref):
    xn = _rmsnorm(x_ref[...], g_ref[...]).astype(BF16)
    qt_ref[...] = (_dot_nt(wq_ref[...], xn) * (LOG2_E * HEAD_DIM ** -0.5)).astype(BF16)
    kt_ref[...] = _dot_nt(wk_ref[...], xn)
    vt = _dot_nt(wv_ref[...], xn)
    vt_ref[...] = vt
    vtb_ref[...] = vt.astype(BF16)
    k = _dot_nt(xn, wk_ref[...])
    for h in range(N_HEADS):
        kh_ref[h] = k[:, h * HEAD_DIM:(h + 1) * HEAD_DIM].astype(BF16)
    lf_ref[...] = _log_sigmoid(_dot_nt(xn, wfp_ref[...]) + bf_ref[...])
    lft_ref[...] = _log_sigmoid(_dot_nt(wf_ref[...], xn) + bft_ref[...])


def _qkvf_prompt(x, gain, w_t, wf_pad, bf, bft, tm):
    n, d = x.shape
    feat_spec = pl.BlockSpec((d, tm), lambda i: (0, i))
    w_spec = lambda blk: pl.BlockSpec((d, d), lambda i: (blk, 0))
    feat = lambda dt: jax.ShapeDtypeStruct((d, n), dt)
    return pl.pallas_call(
        _qkvf_prompt_kernel,
        grid=(n // tm,),
        in_specs=[pl.BlockSpec((tm, d), lambda i: (i, 0)), _full_spec((1, d)),
                  w_spec(0), w_spec(1), w_spec(2),
                  pl.BlockSpec((N_HEADS, d), lambda i: (3 * d // N_HEADS, 0)),
                  _full_spec((LANES, d)), _full_spec((1, LANES)), _full_spec((N_HEADS, 1))],
        out_specs=[feat_spec, pl.BlockSpec((N_HEADS, tm, HEAD_DIM), lambda i: (0, i, 0)),
                   feat_spec, feat_spec, feat_spec,
                   pl.BlockSpec((tm, LANES), lambda i: (i, 0)),
                   pl.BlockSpec((N_HEADS, tm), lambda i: (0, i))],
        out_shape=[feat(BF16), jax.ShapeDtypeStruct((N_HEADS, n, HEAD_DIM), BF16),
                   feat(F32), feat(F32), feat(BF16),
                   jax.ShapeDtypeStruct((n, LANES), F32), jax.ShapeDtypeStruct((N_HEADS, n), F32)],
        compiler_params=_params("parallel"),
        name="qkvf_proj_prompt",
    )(x, gain, w_t, w_t, w_t, w_t, wf_pad, bf, bft)


def _qkvf_sample_kernel(x_ref, g_ref, wq_ref, wk_ref, wv_ref, wfp_ref, bf_ref, q_ref, k_ref, v_ref, lf_ref):
    xn = _rmsnorm(x_ref[...], g_ref[...]).astype(BF16)
    q_ref[...] = _dot_nt(xn, wq_ref[...]) * (HEAD_DIM ** -0.5)
    k_ref[...] = _dot_nt(xn, wk_ref[...])
    v_ref[...] = _dot_nt(xn, wv_ref[...])
    lf_ref[...] = _log_sigmoid(_dot_nt(xn, wfp_ref[...]) + bf_ref[...])


def _qkvf_sample(x, gain, w_t, wf_pad, bf, tm):
    n, d = x.shape
    row_spec = pl.BlockSpec((tm, d), lambda i: (i, 0))
    w_spec = lambda blk: pl.BlockSpec((d, d), lambda i: (blk, 0))
    rows = jax.ShapeDtypeStruct((n, d), F32)
    return pl.pallas_call(
        _qkvf_sample_kernel,
        grid=(n // tm,),
        in_specs=[row_spec, _full_spec((1, d)), w_spec(0), w_spec(1), w_spec(2),
                  _full_spec((LANES, d)), _full_spec((1, LANES))],
        out_specs=[row_spec, row_spec, row_spec, pl.BlockSpec((tm, LANES), lambda i: (i, 0))],
        out_shape=[rows, rows, rows, jax.ShapeDtypeStruct((n, LANES), F32)],
        compiler_params=_params("parallel"),
        name="qkvf_proj_sample",
    )(x, gain, w_t, w_t, w_t, wf_pad, bf)


def _key_aug_kernel(lf_ref, kh_ref, ka_ref, c_ref):
    @pl.when(pl.program_id(0) == 0)
    def _():
        c_ref[...] = jnp.zeros_like(c_ref)

    t = lf_ref.shape[0]
    row = lax.broadcasted_iota(jnp.int32, (t, t), 0)
    col = lax.broadcasted_iota(jnp.int32, (t, t), 1)
    fc = _dot_exact_rhs((col <= row).astype(BF16), lf_ref[...]) + c_ref[...]
    c_ref[...] = fc[t - 1:t, :]
    lane = lax.broadcasted_iota(jnp.int32, (t, HEAD_DIM), 1)
    for h in range(N_HEADS):
        terms = _split3(jnp.broadcast_to(-LOG2_E * fc[:, h:h + 1], (t, HEAD_DIM)))
        aug = jnp.zeros((t, HEAD_DIM), F32)
        for i in range(N_SPLIT):
            aug = jnp.where(lane == i, terms[i].astype(F32), aug)
        ka_ref[h] = jnp.concatenate([kh_ref[h], aug.astype(BF16)], axis=1)


def _key_aug(lf, kh, t):
    s = lf.shape[0]
    return pl.pallas_call(
        _key_aug_kernel,
        grid=(s // t,),
        in_specs=[pl.BlockSpec((t, LANES), lambda i: (i, 0)),
                  pl.BlockSpec((N_HEADS, t, HEAD_DIM), lambda i: (0, i, 0))],
        out_specs=pl.BlockSpec((N_HEADS, t, 2 * HEAD_DIM), lambda i: (0, i, 0)),
        out_shape=jax.ShapeDtypeStruct((N_HEADS, s, 2 * HEAD_DIM), BF16),
        scratch_shapes=[pltpu.VMEM((1, LANES), F32)],
        compiler_params=_params("arbitrary"),
        name="key_aug_cumsum",
    )(lf, kh)


def _prompt_attn_kernel(qt_ref, ka_ref, vt_ref, o_ref, s_sc, mc_sc, m_sc, acc_sc, *, blk):
    qi = pl.program_id(1)
    ones_rows = (lax.broadcasted_iota(jnp.int32, (HEAD_DIM, blk), 0) < N_SPLIT).astype(BF16)
    qa = [jnp.concatenate([qt_ref[pl.ds(hh * HEAD_DIM, HEAD_DIM), :], ones_rows], axis=0)
          for hh in range(ATTN_HEADS)]
    m_sc[...] = jnp.full_like(m_sc, NEG_BIG)
    acc_sc[...] = jnp.zeros_like(acc_sc)
    kidx = lax.broadcasted_iota(jnp.int32, (blk, blk), 0)
    qidx = lax.broadcasted_iota(jnp.int32, (blk, blk), 1)
    ones_blk = jnp.ones((BF16_SUBLANES, blk), BF16)

    def scores(ki, slot, causal):
        ks = pl.multiple_of(ki * blk, blk)
        for hh in range(ATTN_HEADS):
            s = _dot(ka_ref[hh, pl.ds(ks, blk), :], qa[hh])
            if causal:
                s = jnp.where(kidx <= qidx, s, NEG_BIG)
            s_sc[hh, slot] = s
            mc_sc[hh, slot] = jnp.max(s, axis=0, keepdims=True)

    def softmax_pv(ki, slot):
        ks = pl.multiple_of(ki * blk, blk)
        for hh in range(ATTN_HEADS):
            m_prev = m_sc[hh]
            m_new = jnp.maximum(m_prev, mc_sc[hh, slot])
            alpha = jnp.exp2(m_prev - m_new)
            p = jnp.exp2(s_sc[hh, slot] - m_new).astype(BF16)
            v_blk = jnp.concatenate([vt_ref[pl.ds(hh * HEAD_DIM, HEAD_DIM), pl.ds(ks, blk)], ones_blk], axis=0)
            acc_sc[hh] = alpha * acc_sc[hh] + _dot(v_blk, p)
            m_sc[hh] = m_new

    @pl.when(qi == 0)
    def _():
        scores(0, 0, True)
        softmax_pv(0, 0)

    @pl.when(qi > 0)
    def _():
        scores(0, 0, False)
        n_pairs = (qi - 1) // 2

        def body(k, carry):
            scores(2 * k + 1, 1, False)
            softmax_pv(2 * k, 0)
            scores(2 * k + 2, 0, False)
            softmax_pv(2 * k + 1, 1)
            return carry

        lax.fori_loop(0, n_pairs, body, 0)
        i0 = 2 * n_pairs

        @pl.when(qi - i0 == 2)
        def _():
            scores(i0 + 1, 1, False)
            softmax_pv(i0, 0)
            scores(qi, 0, True)
            softmax_pv(i0 + 1, 1)
            softmax_pv(qi, 0)

        @pl.when(qi - i0 == 1)
        def _():
            scores(qi, 1, True)
            softmax_pv(i0, 0)
            softmax_pv(qi, 1)

    for hh in range(ATTN_HEADS):
        acc = acc_sc[hh]
        o_ref[pl.ds(hh * HEAD_DIM, HEAD_DIM), :] = (acc[:HEAD_DIM] / acc[HEAD_DIM:HEAD_DIM + 1]).astype(o_ref.dtype)


def _prompt_attn(qt, ka, vt, blk):
    d, s = qt.shape
    rows = ATTN_HEADS * HEAD_DIM
    return pl.pallas_call(
        functools.partial(_prompt_attn_kernel, blk=blk),
        grid=(N_HEADS // ATTN_HEADS, s // blk),
        in_specs=[pl.BlockSpec((rows, blk), lambda h, i: (h, i)),
                  pl.BlockSpec((ATTN_HEADS, s, 2 * HEAD_DIM), lambda h, i: (h, 0, 0)),
                  pl.BlockSpec((rows, s), lambda h, i: (h, 0))],
        out_specs=pl.BlockSpec((rows, blk), lambda h, i: (h, i)),
        out_shape=jax.ShapeDtypeStruct((d, s), BF16),
        scratch_shapes=[pltpu.VMEM((ATTN_HEADS, 2, blk, blk), F32), pltpu.VMEM((ATTN_HEADS, 2, 1, blk), F32),
                        pltpu.VMEM((ATTN_HEADS, 1, blk), F32),
                        pltpu.VMEM((ATTN_HEADS, HEAD_DIM + BF16_SUBLANES, blk), F32)],
        compiler_params=_params("parallel", "parallel"),
        name="fox_prompt_attn",
    )(qt, ka, vt)


def _sample_attn_kernel(pt_ref, *refs, n_pp, page, t_new):
    del pt_ref
    k_refs = refs[:n_pp]
    v_refs = refs[n_pp:2 * n_pp]
    lf_refs = refs[2 * n_pp:3 * n_pp]
    q_ref, kn_ref, vn_ref, lfn_ref, o_ref, qbd_sc, m_sc, l_sc, acc_sc, suf_sc = refs[3 * n_pp:]
    j = pl.program_id(1)
    rows = N_HEADS * t_new
    d = N_HEADS * HEAD_DIM
    own = (lax.broadcasted_iota(jnp.int32, (rows, d), 1) // HEAD_DIM
           == lax.broadcasted_iota(jnp.int32, (rows, d), 0) // t_new)

    def online_update(s, pv_fn):
        m_prev = m_sc[...]
        m_new = jnp.maximum(m_prev, jnp.max(s, axis=1, keepdims=True))
        alpha = jnp.exp(m_prev - m_new)
        p = jnp.exp(s - m_new)
        l_sc[...] = alpha * l_sc[...] + jnp.sum(p, axis=1, keepdims=True)
        acc_sc[...] = alpha * acc_sc[...] + pv_fn(p.astype(BF16))
        m_sc[...] = m_new

    def add_head_rows(s, bias):
        return jnp.concatenate(
            [s[h * t_new:(h + 1) * t_new, :] + bias[h:h + 1, :] for h in range(N_HEADS)], axis=0)

    @pl.when(j == 0)
    def _():
        m_sc[...] = jnp.full_like(m_sc, NEG_BIG)
        l_sc[...] = jnp.zeros_like(l_sc)
        acc_sc[...] = jnp.zeros_like(acc_sc)
        suf_sc[...] = jnp.zeros_like(suf_sc)
        q_rep = jnp.concatenate([q_ref[...]] * N_HEADS, axis=0)
        qbd_sc[...] = jnp.where(own, q_rep, 0.0).astype(BF16)
        r8 = lax.broadcasted_iota(jnp.int32, (t_new, t_new), 0)
        c8 = lax.broadcasted_iota(jnp.int32, (t_new, t_new), 1)
        cum = _dot_exact_rhs((c8 <= r8).astype(BF16), lfn_ref[...])
        cum_pad = jnp.concatenate([cum, jnp.zeros((LANES - t_new, LANES), F32)], axis=0)
        eye = (lax.broadcasted_iota(jnp.int32, (N_HEADS, LANES), 0)
               == lax.broadcasted_iota(jnp.int32, (N_HEADS, LANES), 1)).astype(BF16)
        hi, mid, lo = _split3(cum_pad)
        cum_t = (_dot_nt(eye, hi) + _dot_nt(eye, mid)) + _dot_nt(eye, lo)
        zpad = jnp.zeros((LANES - t_new, d), F32)
        kn = jnp.concatenate([kn_ref[...], zpad], axis=0).astype(BF16)
        vn = jnp.concatenate([vn_ref[...], zpad], axis=0).astype(BF16)
        s = add_head_rows(_dot_nt(qbd_sc[...], kn), -cum_t)
        qt = lax.broadcasted_iota(jnp.int32, (rows, LANES), 0) % t_new
        kt = lax.broadcasted_iota(jnp.int32, (rows, LANES), 1)
        s = jnp.where(kt <= qt, s, NEG_BIG)
        online_update(s, lambda p: _dot(p, vn))

    rp = lax.broadcasted_iota(jnp.int32, (page, page), 0)
    cp = lax.broadcasted_iota(jnp.int32, (page, page), 1)
    newer = (rp > cp).astype(BF16)
    carry = suf_sc[...]
    suf_pages = [None] * n_pp
    for i in reversed(range(n_pp)):
        lf = lf_refs[i][...]
        suf_pages[i] = _dot_exact_lhs(lf, newer) + carry
        carry = carry + jnp.sum(lf, axis=1, keepdims=True)
    suf_sc[...] = carry
    suf = jnp.concatenate(suf_pages, axis=1)

    kt_pages = jnp.concatenate([r[...] for r in k_refs], axis=1).astype(BF16)
    vt_pages = jnp.concatenate([r[...] for r in v_refs], axis=1).astype(BF16)
    s = add_head_rows(_dot(qbd_sc[...], kt_pages), suf)
    online_update(s, lambda p: _dot_nt(p, vt_pages))

    @pl.when(j == pl.num_programs(1) - 1)
    def _():
        out = jnp.where(own, acc_sc[...] / l_sc[...], 0.0)
        o = out[0:t_new, :]
        for h in range(1, N_HEADS):
            o = o + out[h * t_new:(h + 1) * t_new, :]
        o_ref[...] = o


def _sample_attn(page_table, cache_kt, cache_vt, cache_lft, q_s, k_s, v_s, lf_s, n_pp):
    n_seq, n_pages = page_table.shape
    _, d, page = cache_kt.shape
    t_new = q_s.shape[0] // n_seq
    n_grp = n_pages // n_pp

    def page_idx(i):
        return lambda b, j, pt: (pt[b, (n_grp - 1 - j) * n_pp + i], 0, 0)

    kv_specs = [pl.BlockSpec((None, d, page), page_idx(i)) for i in range(n_pp)]
    lf_specs = [pl.BlockSpec((None, N_HEADS, page), page_idx(i)) for i in range(n_pp)]
    tok_spec = pl.BlockSpec((t_new, d), lambda b, j, pt: (b, 0))
    rows = N_HEADS * t_new
    grid_spec = pltpu.PrefetchScalarGridSpec(
        num_scalar_prefetch=1,
        grid=(n_seq, n_grp),
        in_specs=kv_specs + kv_specs + lf_specs + [
            tok_spec, tok_spec, tok_spec, pl.BlockSpec((t_new, LANES), lambda b, j, pt: (b, 0))],
        out_specs=tok_spec,
        scratch_shapes=[pltpu.VMEM((rows, d), BF16), pltpu.VMEM((rows, 1), F32), pltpu.VMEM((rows, 1), F32),
                        pltpu.VMEM((rows, d), F32), pltpu.VMEM((N_HEADS, 1), F32)],
    )
    return pl.pallas_call(
        functools.partial(_sample_attn_kernel, n_pp=n_pp, page=page, t_new=t_new),
        grid_spec=grid_spec,
        out_shape=jax.ShapeDtypeStruct((n_seq * t_new, d), F32),
        compiler_params=_params("parallel", "arbitrary"),
        name="fox_sample_attn",
    )(page_table, *([cache_kt] * n_pp), *([cache_vt] * n_pp), *([cache_lft] * n_pp), q_s, k_s, v_s, lf_s)


def _oproj_kernel(a_ref, x_ref, wo_ref, o_ref, *, feature_major):
    a = a_ref[...].astype(BF16)
    if feature_major:
        proj = lax.dot_general(a, wo_ref[...], _TN, preferred_element_type=F32)
    else:
        proj = _dot(a, wo_ref[...])
    o_ref[...] = x_ref[...] + proj


def _oproj(attn, x, wo, tm, feature_major):
    n, d = x.shape
    row_spec = pl.BlockSpec((tm, d), lambda i: (i, 0))
    attn_spec = pl.BlockSpec((d, tm), lambda i: (0, i)) if feature_major else row_spec
    return pl.pallas_call(
        functools.partial(_oproj_kernel, feature_major=feature_major),
        grid=(n // tm,),
        in_specs=[attn_spec, row_spec, _full_spec((d, d))],
        out_specs=row_spec,
        out_shape=jax.ShapeDtypeStruct((n, d), F32),
        compiler_params=_params("parallel"),
        name="attn_out_proj",
    )(attn, x, wo)


def _swiglu_act(x_bf16, wg_ref, wu_ref):
    gate = _dot(x_bf16, wg_ref[...])
    return gate * jax.nn.sigmoid(gate) * _dot(x_bf16, wu_ref[...])


def _ffn_kernel(x_ref, g_ref, wg_ref, wu_ref, wd_ref, o_ref, xn_sc):
    j = pl.program_id(1)

    @pl.when(j == 0)
    def _():
        x = x_ref[...]
        xn_sc[...] = _rmsnorm(x, g_ref[...]).astype(BF16)
        o_ref[...] = x

    act = _swiglu_act(xn_sc[...], wg_ref, wu_ref).astype(BF16)
    o_ref[...] += _dot(act, wd_ref[...])


def _ffn(x, gain, w_gate_up, w_down, tm, tf):
    n, d = x.shape
    n_f = w_down.shape[0] // tf
    row_spec = pl.BlockSpec((tm, d), lambda i, j: (i, 0), pipeline_mode=pl.Buffered(1))
    return pl.pallas_call(
        _ffn_kernel,
        grid=(n // tm, n_f),
        in_specs=[row_spec, pl.BlockSpec((1, d), lambda i, j: (0, 0)),
                  pl.BlockSpec((d, tf), lambda i, j: (0, j)),
                  pl.BlockSpec((d, tf), lambda i, j: (0, j + n_f)),
                  pl.BlockSpec((tf, d), lambda i, j: (j, 0))],
        out_specs=row_spec,
        out_shape=jax.ShapeDtypeStruct((n, d), F32),
        scratch_shapes=[pltpu.VMEM((tm, d), BF16)],
        compiler_params=_params("parallel", "arbitrary"),
        name="dense_swiglu",
    )(x, gain, w_gate_up, w_gate_up, w_down)


def _pool_mix(ext_ref, x, u, pos0, wgrp_ref, scale_ref, o_ref, n_rows):
    gdim = u.shape[1] // len(POOL_WINDOWS)
    pos = pos0 + lax.broadcasted_iota(jnp.int32, (n_rows, 1), 0)
    for g, w in enumerate(POOL_WINDOWS):
        cols = pl.ds(g * gdim, gdim)
        win = ext_ref[pl.ds(POOL_PAD, n_rows), cols]
        for i in range(1, w):
            win = win + ext_ref[pl.ds(POOL_PAD - i, n_rows), cols]
        count = jnp.minimum(pos + 1, w).astype(F32)
        pooled = (win / count - u[:, g * gdim:(g + 1) * gdim]).astype(BF16)
        mixed = _dot(pooled, wgrp_ref[g]) * scale_ref[:, cols]
        o_ref[:, cols] = x[:, g * gdim:(g + 1) * gdim] + mixed


def _pool_prompt_kernel(x_ref, g_ref, wgrp_ref, scale_ref, o_ref, tail_ref, ext_sc, *, tm):
    i = pl.program_id(0)

    @pl.when(i == 0)
    def _():
        ext_sc[pl.ds(0, POOL_PAD), :] = jnp.zeros((POOL_PAD, ext_sc.shape[1]), F32)

    @pl.when(i > 0)
    def _():
        ext_sc[pl.ds(0, POOL_PAD), :] = ext_sc[pl.ds(tm, POOL_PAD), :]

    x = x_ref[...]
    u = _rmsnorm(x, g_ref[...])
    ext_sc[pl.ds(POOL_PAD, tm), :] = u
    tail_ref[...] = u[tm - POOL_PAD:, :]
    _pool_mix(ext_sc, x, u, i * tm, wgrp_ref, scale_ref, o_ref, tm)


def _pool_prompt(x, gain, w_group, scale, tm):
    n, d = x.shape
    gdim = d // len(POOL_WINDOWS)
    row_spec = pl.BlockSpec((tm, d), lambda i: (i, 0))
    return pl.pallas_call(
        functools.partial(_pool_prompt_kernel, tm=tm),
        grid=(n // tm,),
        in_specs=[row_spec, pl.BlockSpec((1, d), lambda i: (0, 0)),
                  pl.BlockSpec((len(POOL_WINDOWS), gdim, gdim), lambda i: (0, 0, 0)),
                  pl.BlockSpec((1, d), lambda i: (0, 0))],
        out_specs=[row_spec, pl.BlockSpec((POOL_PAD, d), lambda i: (0, 0))],
        out_shape=[jax.ShapeDtypeStruct((n, d), F32), jax.ShapeDtypeStruct((POOL_PAD, d), F32)],
        scratch_shapes=[pltpu.VMEM((POOL_PAD + tm, d), F32)],
        compiler_params=_params("arbitrary"),
        name="pool_mix_prompt",
    )(x, gain, w_group, scale)


def _pool_sample_kernel(x_ref, st_ref, g_ref, wgrp_ref, scale_ref, o_ref, hist_ref, ext_sc, *, t_new, past_len):
    x = x_ref[...]
    u = _rmsnorm(x, g_ref[...])
    ext_sc[pl.ds(0, POOL_PAD), :] = st_ref[...]
    ext_sc[pl.ds(POOL_PAD, t_new), :] = u
    hist_ref[...] = ext_sc[pl.ds(t_new, POOL_PAD), :]
    _pool_mix(ext_sc, x, u, past_len, wgrp_ref, scale_ref, o_ref, t_new)


def _pool_sample(x, state_pad, gain, w_group, scale, t_new, past_len):
    n, d = x.shape
    gdim = d // len(POOL_WINDOWS)
    row_spec = pl.BlockSpec((t_new, d), lambda b: (b, 0))
    st_spec = pl.BlockSpec((None, POOL_PAD, d), lambda b: (b, 0, 0))
    return pl.pallas_call(
        functools.partial(_pool_sample_kernel, t_new=t_new, past_len=past_len),
        grid=(n // t_new,),
        in_specs=[row_spec, st_spec, pl.BlockSpec((1, d), lambda b: (0, 0)),
                  pl.BlockSpec((len(POOL_WINDOWS), gdim, gdim), lambda b: (0, 0, 0)),
                  pl.BlockSpec((1, d), lambda b: (0, 0))],
        out_specs=[row_spec, st_spec],
        out_shape=[jax.ShapeDtypeStruct((n, d), F32), jax.ShapeDtypeStruct(state_pad.shape, F32)],
        scratch_shapes=[pltpu.VMEM((POOL_PAD + t_new, d), F32)],
        compiler_params=_params("parallel"),
        name="pool_mix_sample",
    )(x, state_pad, gain, w_group, scale)


def _moe_kernel(x_ref, g_ref, wrt_ref, wg_ref, wu_ref, wd_ref, gf_ref, o_ref,
                xn_sc, gate_sc, rank_sc, xe_sc, ge_sc, ye_sc, *, c0):
    e = pl.program_id(1)
    j = pl.program_id(2)
    tm = x_ref.shape[0]
    e_rows = gate_sc.shape[0]

    @pl.when((e == 0) & (j == 0))
    def _():
        xn = _rmsnorm(x_ref[...], g_ref[...])
        xh, xm, xl = _split3(xn)
        xn_sc[...] = xh
        wh, wm, wl = wrt_ref[0], wrt_ref[1], wrt_ref[2]
        logits = ((_dot_nt(wl, xh) + _dot_nt(wm, xm) + _dot_nt(wh, xl))
                  + (_dot_nt(wm, xh) + _dot_nt(wh, xm))) + _dot_nt(wh, xh)
        erow = lax.broadcasted_iota(jnp.int32, (e_rows, tm), 0)
        logits = jnp.where(erow < N_EXPERTS, logits, -jnp.inf)
        top1 = jnp.max(logits, axis=0, keepdims=True)
        idx1 = jnp.min(jnp.where(logits == top1, erow, e_rows), axis=0, keepdims=True)
        rest = jnp.where(erow == idx1, -jnp.inf, logits)
        top2 = jnp.max(rest, axis=0, keepdims=True)
        idx2 = jnp.min(jnp.where(rest == top2, erow, e_rows), axis=0, keepdims=True)
        e2 = jnp.exp(top2 - top1)
        denom = 1.0 + e2
        gate_sc[...] = jnp.where(erow == idx1, 1.0 / denom, 0.0) + jnp.where(erow == idx2, e2 / denom, 0.0)
        routed = (erow == idx1) | (erow == idx2)
        earlier = jnp.where(lax.broadcasted_iota(jnp.int32, (tm, tm), 0)
                            < lax.broadcasted_iota(jnp.int32, (tm, tm), 1), 1.0, 0.0).astype(BF16)
        rank = _dot(jnp.where(routed, 1.0, 0.0).astype(BF16), earlier)
        rank_sc[...] = jnp.where(routed, rank, -1.0)
        o_ref[...] = x_ref[...]

    rank_e = rank_sc[pl.ds(e, 1), :]
    count = jnp.sum(jnp.where(rank_e >= 0.0, 1.0, 0.0)).astype(jnp.int32)
    n_extra = lax.shift_right_logical(jnp.maximum(count - c0, 0) + (MOE_CHUNK - 1), MOE_CHUNK.bit_length() - 1)

    def for_chunks(fn):
        fn(0, c0)

        def body(k, carry):
            fn(pl.multiple_of(c0 + k * MOE_CHUNK, BF16_SUBLANES), MOE_CHUNK)
            return carry

        lax.fori_loop(0, n_extra, body, 0)

    def selected(base, rows, value):
        slot = (base + lax.broadcasted_iota(jnp.int32, (rows, 1), 0)).astype(F32)
        return jnp.where(rank_e == slot, value, 0.0)

    def selection(base, rows):
        return selected(base, rows, 1.0).astype(BF16)

    def gather(base, rows):
        xe_sc[pl.ds(base, rows), :] = _dot(selection(base, rows), xn_sc[...]).astype(BF16)
        gate_e = gate_sc[pl.ds(e, 1), :]
        ge_sc[pl.ds(base, rows), :] = jnp.sum(selected(base, rows, gate_e), axis=1, keepdims=True)
        ye_sc[pl.ds(base, rows), :] = jnp.zeros((rows, ye_sc.shape[1]), F32)

    def expert(base, rows):
        act = _swiglu_act(xe_sc[pl.ds(base, rows), :], wg_ref, wu_ref) * ge_sc[pl.ds(base, rows), :]
        ye_sc[pl.ds(base, rows), :] += _dot(act.astype(BF16), wd_ref[...])

    def scatter(base, rows):
        y = ye_sc[pl.ds(base, rows), :].astype(BF16)
        o_ref[...] += lax.dot_general(selection(base, rows), y, _TN, preferred_element_type=F32)

    @pl.when(j == 0)
    def _():
        for_chunks(gather)

    for_chunks(expert)

    @pl.when(j == pl.num_programs(2) - 1)
    def _():
        for_chunks(scatter)

    @pl.when((e == pl.num_programs(1) - 1) & (j == pl.num_programs(2) - 1))
    def _():
        o_ref[...] = _rmsnorm(o_ref[...], gf_ref[...])


def _moe(x, gain, w_router_t3, w_gate_up, w_down, gain_final, tm, tf):
    n, d = x.shape
    n_e, d_ff, _ = w_down.shape
    n_f = d_ff // tf
    assert tm % MOE_CHUNK == 0
    mean_rows = tm * 2 // n_e
    c0 = max(MOE_CHUNK, -(-(mean_rows + mean_rows // 8) // BF16_SUBLANES) * BF16_SUBLANES)
    cap = c0 + -(-(tm - c0) // MOE_CHUNK) * MOE_CHUNK
    row_spec = pl.BlockSpec((tm, d), lambda i, e, j: (i, 0), pipeline_mode=pl.Buffered(1))
    vec_spec = pl.BlockSpec((1, d), lambda i, e, j: (0, 0))
    return pl.pallas_call(
        functools.partial(_moe_kernel, c0=c0),
        grid=(n // tm, n_e, n_f),
        in_specs=[row_spec, vec_spec,
                  pl.BlockSpec((N_SPLIT, LANES, d), lambda i, e, j: (0, 0, 0)),
                  pl.BlockSpec((None, d, tf), lambda i, e, j: (e, 0, j)),
                  pl.BlockSpec((None, d, tf), lambda i, e, j: (e, 0, j + n_f)),
                  pl.BlockSpec((None, tf, d), lambda i, e, j: (e, j, 0)),
                  vec_spec],
        out_specs=row_spec,
        out_shape=jax.ShapeDtypeStruct((n, d), F32),
        scratch_shapes=[pltpu.VMEM((tm, d), BF16), pltpu.VMEM((LANES, tm), F32), pltpu.VMEM((LANES, tm), F32),
                        pltpu.VMEM((cap, d), BF16), pltpu.VMEM((cap, 1), F32), pltpu.VMEM((cap, d), F32)],
        compiler_params=_params("parallel", "arbitrary", "arbitrary"),
        name="moe_swiglu_final_norm",
    )(x, gain, w_router_t3, w_gate_up, w_gate_up, w_down, gain_final)


def _row_tile(n, target):
    t = min(n, target)
    while n % t:
        t //= 2
    return t


def kernel(x_prompt, x_sample, cache_k, cache_v, cache_logf, state_pool, page_table, l0_norm_attn, l0_w_qkvf, l0_b_f, l0_w_o, l0_norm_ffn, l0_w_gate_up, l0_w_down, l1_norm_pool, l1_w_group, l1_pool_scale, l1_norm_ffn, l1_w_router, l1_w_gate_up, l1_w_down, final_norm):
    b_p, s_p, d = x_prompt.shape
    b_s, t_new, _ = x_sample.shape
    assert b_p == 1 and d == N_HEADS * HEAD_DIM
    n_phys, page = cache_k.shape[:2]
    past_len = page_table.shape[1] * page
    d_ff = l0_w_down.shape[0]

    row = lambda v: v.reshape(1, -1).astype(F32)
    w_t = l0_w_qkvf.T.astype(BF16)
    wf_pad = jnp.pad(w_t[3 * d:], ((0, LANES - N_HEADS), (0, 0)))
    bf = jnp.pad(l0_b_f, (0, LANES - N_HEADS)).reshape(1, LANES)
    bft = l0_b_f.reshape(N_HEADS, 1)
    wo = l0_w_o.astype(BF16)
    tf = d_ff // 2 if d_ff % (2 * MXU_WIDTH) == 0 else d_ff
    w0_gu = l0_w_gate_up.astype(BF16)
    w0_d = l0_w_down.astype(BF16)
    wgrp = l1_w_group.astype(BF16)
    wr_pad = jnp.pad(l1_w_router.T, ((0, LANES - N_EXPERTS), (0, 0)))
    wr_hi = wr_pad.astype(BF16)
    wr_r1 = wr_pad - wr_hi.astype(F32)
    wr_mid = wr_r1.astype(BF16)
    wr_lo = (wr_r1 - wr_mid.astype(F32)).astype(BF16)
    wr3 = jnp.stack([wr_hi, wr_mid, wr_lo])
    w1_gu = l1_w_gate_up.astype(BF16)
    w1_d = l1_w_down.astype(BF16)

    xp = x_prompt.reshape(s_p, d)
    xs = x_sample.reshape(b_s * t_new, d)
    n_s = b_s * t_new

    blk = _row_tile(s_p, 512)
    qt_p, kh_p, kt_p, vt_p, vtb_p, lf_p, lft_p = _qkvf_prompt(xp, row(l0_norm_attn), w_t, wf_pad, bf, bft, blk)
    q_s, k_s, v_s, lf_s = _qkvf_sample(xs, row(l0_norm_attn), w_t, wf_pad, bf, _row_tile(n_s, 512))

    attn_p = _prompt_attn(qt_p, _key_aug(lf_p, kh_p, blk), vtb_p, _row_tile(s_p, ATTN_BLOCK))

    feat_major = lambda c: jnp.transpose(c, (0, 2, 3, 1)).reshape(n_phys, d, page)
    n_pp = math.gcd(page_table.shape[1], PAGES_PER_STEP)
    attn_s = _sample_attn(page_table, feat_major(cache_k), feat_major(cache_v), jnp.swapaxes(cache_logf, 1, 2),
                          q_s, k_s, v_s, lf_s, n_pp)

    hp = _oproj(attn_p, xp, wo, blk, True)
    hs = _oproj(attn_s, xs, wo, _row_tile(n_s, 512), False)

    hp = _ffn(hp, row(l0_norm_ffn), w0_gu, w0_d, _row_tile(s_p, 1024), tf)
    hs = _ffn(hs, row(l0_norm_ffn), w0_gu, w0_d, _row_tile(n_s, 1024), tf)

    hp, tail_p = _pool_prompt(hp, row(l1_norm_pool), wgrp, row(l1_pool_scale), _row_tile(s_p, 512))
    state_pad = jnp.pad(state_pool, ((0, 0), (POOL_PAD - POOL_HIST, 0), (0, 0)))
    hs, hist_s = _pool_sample(hs, state_pad, row(l1_norm_pool), wgrp, row(l1_pool_scale), t_new, past_len)

    yp = _moe(hp, row(l1_norm_ffn), wr3, w1_gu, w1_d, row(final_norm), _row_tile(s_p, 1024), tf)
    ys = _moe(hs, row(l1_norm_ffn), wr3, w1_gu, w1_d, row(final_norm), _row_tile(n_s, 1024), tf)

    heads_t = lambda a: jnp.transpose(a.reshape(1, N_HEADS, HEAD_DIM, s_p), (0, 3, 1, 2))
    heads = lambda a: a.reshape(b_s, t_new, N_HEADS, HEAD_DIM)
    return (yp.reshape(b_p, s_p, d), ys.reshape(b_s, t_new, d),
            heads_t(kt_p), heads_t(vt_p), jnp.transpose(lft_p.reshape(1, N_HEADS, s_p), (0, 2, 1)),
            tail_p[POOL_PAD - POOL_HIST:].reshape(b_p, POOL_HIST, d),
            heads(k_s), heads(v_s), lf_s[:, :N_HEADS].reshape(b_s, t_new, N_HEADS),
            hist_s[:, POOL_PAD - POOL_HIST:, :])
```

```python
import functools
import math

import jax
import jax.numpy as jnp
from jax import lax
from jax.experimental import pallas as pl
from jax.experimental.pallas import tpu as pltpu

N_HEADS = 16
HEAD_DIM = 64
RMS_EPS = 1e-6
POOL_WINDOWS = (2, 4, 8, 16)
POOL_HIST = max(POOL_WINDOWS) - 1
POOL_PAD = POOL_HIST + 1
N_EXPERTS = 8
LANES = 128
BF16_SUBLANES = 16
MXU_WIDTH = 256
N_SPLIT = 3
NEG_BIG = -1e30
LOG2_E = 1.4426950408889634
MOE_CHUNK = 128
ATTN_HEADS = 2
ATTN_BLOCK = 512
PAGES_PER_STEP = 16
VMEM_LIMIT_BYTES = 56 * 1024 * 1024

F32 = jnp.float32
BF16 = jnp.bfloat16
_NT = (((1,), (1,)), ((), ()))
_TN = (((0,), (0,)), ((), ()))


def _params(*sem):
    return pltpu.CompilerParams(dimension_semantics=sem, vmem_limit_bytes=VMEM_LIMIT_BYTES)


def _rmsnorm(x, gain):
    ms = jnp.mean(x * x, axis=-1, keepdims=True)
    return x * lax.rsqrt(ms + RMS_EPS) * gain


def _split3(x):
    hi = x.astype(BF16)
    r1 = x - hi.astype(F32)
    mid = r1.astype(BF16)
    lo = (r1 - mid.astype(F32)).astype(BF16)
    return hi, mid, lo


def _dot(a, b):
    return jnp.dot(a, b, preferred_element_type=F32)


def _dot_nt(a, b):
    return lax.dot_general(a, b, _NT, preferred_element_type=F32)


def _dot_exact_rhs(a_bf16, x_f32):
    hi, mid, lo = _split3(x_f32)
    return (_dot(a_bf16, hi) + _dot(a_bf16, mid)) + _dot(a_bf16, lo)


def _dot_exact_lhs(x_f32, a_bf16):
    hi, mid, lo = _split3(x_f32)
    return (_dot(hi, a_bf16) + _dot(mid, a_bf16)) + _dot(lo, a_bf16)


def _log_sigmoid(z):
    return jnp.minimum(z, 0.0) - jnp.log1p(jnp.exp(-jnp.abs(z)))


def _full_spec(shape):
    return pl.BlockSpec(shape, lambda *_: (0,) * len(shape))


def _qkvf_prompt_kernel(x_ref, g_ref, wq_ref, wk_ref, wv_ref, wf_ref, wfp_ref, bf_ref, bft_ref,
                        qt_ref, kh_ref, kt_ref, vt_ref, vtb_ref, lf_ref, lft_ref):
    xn = _rmsnorm(x_ref[...], g_ref[...]).astype(BF16)
    qt_ref[...] = (_dot_nt(wq_ref[...], xn) * (LOG2_E * HEAD_DIM ** -0.5)).astype(BF16)
    kt_ref[...] = _dot_nt(wk_ref[...], xn)
    vt = _dot_nt(wv_ref[...], xn)
    vt_ref[...] = vt
    vtb_ref[...] = vt.astype(BF16)
    k = _dot_nt(xn, wk_ref[...])
    for h in range(N_HEADS):
        kh_ref[h] = k[:, h * HEAD_DIM:(h + 1) * HEAD_DIM].astype(BF16)
    lf_ref[...] = _log_sigmoid(_dot_nt(xn, wfp_ref[...]) + bf_ref[...])
    lft_ref[...] = _log_sigmoid(_dot_nt(wf_ref[...], xn) + bft_ref[...])


def _qkvf_prompt(x, gain, w_t, wf_pad, bf, bft, tm):
    n, d = x.shape
    feat_spec = pl.BlockSpec((d, tm), lambda i: (0, i))
    w_spec = lambda blk: pl.BlockSpec((d, d), lambda i: (blk, 0))
    feat = lambda dt: jax.ShapeDtypeStruct((d, n), dt)
    return pl.pallas_call(
        _qkvf_prompt_kernel,
        grid=(n // tm,),
        in_specs=[pl.BlockSpec((tm, d), lambda i: (i, 0)), _full_spec((1, d)),
                  w_spec(0), w_spec(1), w_spec(2),
                  pl.BlockSpec((N_HEADS, d), lambda i: (3 * d // N_HEADS, 0)),
                  _full_spec((LANES, d)), _full_spec((1, LANES)), _full_spec((N_HEADS, 1))],
        out_specs=[feat_spec, pl.BlockSpec((N_HEADS, tm, HEAD_DIM), lambda i: (0, i, 0)),
                   feat_spec, feat_spec, feat_spec,
                   pl.BlockSpec((tm, LANES), lambda i: (i, 0)),
                   pl.BlockSpec((N_HEADS, tm), lambda i: (0, i))],
        out_shape=[feat(BF16), jax.ShapeDtypeStruct((N_HEADS, n, HEAD_DIM), BF16),
                   feat(F32), feat(F32), feat(BF16),
                   jax.ShapeDtypeStruct((n, LANES), F32), jax.ShapeDtypeStruct((N_HEADS, n), F32)],
        compiler_params=_params("parallel"),
        name="qkvf_proj_prompt",
    )(x, gain, w_t, w_t, w_t, w_t, wf_pad, bf, bft)


def _qkvf_sample_kernel(x_ref, g_ref, wq_ref, wk_ref, wv_ref, wfp_ref, bf_ref, q_ref, k_ref, v_ref, lf_ref):
    xn = _rmsnorm(x_ref[...], g_ref[...]).astype(BF16)
    q_ref[...] = _dot_nt(xn, wq_ref[...]) * (HEAD_DIM ** -0.5)
    k_ref[...] = _dot_nt(xn, wk_ref[...])
    v_ref[...] = _dot_nt(xn, wv_ref[...])
    lf_ref[...] = _log_sigmoid(_dot_nt(xn, wfp_ref[...]) + bf_ref[...])


def _qkvf_sample(x, gain, w_t, wf_pad, bf, tm):
    n, d = x.shape
    row_spec = pl.BlockSpec((tm, d), lambda i: (i, 0))
    w_spec = lambda blk: pl.BlockSpec((d, d), lambda i: (blk, 0))
    rows = jax.ShapeDtypeStruct((n, d), F32)
    return pl.pallas_call(
        _qkvf_sample_kernel,
        grid=(n // tm,),
        in_specs=[row_spec, _full_spec((1, d)), w_spec(0), w_spec(1), w_spec(2),
                  _full_spec((LANES, d)), _full_spec((1, LANES))],
        out_specs=[row_spec, row_spec, row_spec, pl.BlockSpec((tm, LANES), lambda i: (i, 0))],
        out_shape=[rows, rows, rows, jax.ShapeDtypeStruct((n, LANES), F32)],
        compiler_params=_params("parallel"),
        name="qkvf_proj_sample",
    )(x, gain, w_t, w_t, w_t, wf_pad, bf)


def _key_aug_kernel(lf_ref, kh_ref, ka_ref, c_ref):
    @pl.when(pl.program_id(0) == 0)
    def _():
        c_ref[...] = jnp.zeros_like(c_ref)

    t = lf_ref.shape[0]
    row = lax.broadcasted_iota(jnp.int32, (t, t), 0)
    col = lax.broadcasted_iota(jnp.int32, (t, t), 1)
    fc = _dot_exact_rhs((col <= row).astype(BF16), lf_ref[...]) + c_ref[...]
    c_ref[...] = fc[t - 1:t, :]
    lane = lax.broadcasted_iota(jnp.int32, (t, HEAD_DIM), 1)
    for h in range(N_HEADS):
        terms = _split3(jnp.broadcast_to(-LOG2_E * fc[:, h:h + 1], (t, HEAD_DIM)))
        aug = jnp.zeros((t, HEAD_DIM), F32)
        for i in range(N_SPLIT):
            aug = jnp.where(lane == i, terms[i].astype(F32), aug)
        ka_ref[h] = jnp.concatenate([kh_ref[h], aug.astype(BF16)], axis=1)


def _key_aug(lf, kh, t):
    s = lf.shape[0]
    return pl.pallas_call(
        _key_aug_kernel,
        grid=(s // t,),
        in_specs=[pl.BlockSpec((t, LANES), lambda i: (i, 0)),
                  pl.BlockSpec((N_HEADS, t, HEAD_DIM), lambda i: (0, i, 0))],
        out_specs=pl.BlockSpec((N_HEADS, t, 2 * HEAD_DIM), lambda i: (0, i, 0)),
        out_shape=jax.ShapeDtypeStruct((N_HEADS, s, 2 * HEAD_DIM), BF16),
        scratch_shapes=[pltpu.VMEM((1, LANES), F32)],
        compiler_params=_params("arbitrary"),
        name="key_aug_cumsum",
    )(lf, kh)


def _prompt_attn_kernel(qt_ref, ka_ref, vt_ref, o_ref, s_sc, mc_sc, m_sc, acc_sc, *, blk):
    qi = pl.program_id(1)
    ones_rows = (lax.broadcasted_iota(jnp.int32, (HEAD_DIM, blk), 0) < N_SPLIT).astype(BF16)
    qa = [jnp.concatenate([qt_ref[pl.ds(hh * HEAD_DIM, HEAD_DIM), :], ones_rows], axis=0)
          for hh in range(ATTN_HEADS)]
    m_sc[...] = jnp.full_like(m_sc, NEG_BIG)
    acc_sc[...] = jnp.zeros_like(acc_sc)
    kidx = lax.broadcasted_iota(jnp.int32, (blk, blk), 0)
    qidx = lax.broadcasted_iota(jnp.int32, (blk, blk), 1)
    ones_blk = jnp.ones((BF16_SUBLANES, blk), BF16)

    def scores(ki, slot, causal):
        ks = pl.multiple_of(ki * blk, blk)
        for hh in range(ATTN_HEADS):
            s = _dot(ka_ref[hh, pl.ds(ks, blk), :], qa[hh])
            if causal:
                s = jnp.where(kidx <= qidx, s, NEG_BIG)
            s_sc[hh, slot] = s
            mc_sc[hh, slot] = jnp.max(s, axis=0, keepdims=True)

    def softmax_pv(ki, slot):
        ks = pl.multiple_of(ki * blk, blk)
        for hh in range(ATTN_HEADS):
            m_prev = m_sc[hh]
            m_new = jnp.maximum(m_prev, mc_sc[hh, slot])
            alpha = jnp.exp2(m_prev - m_new)
            p = jnp.exp2(s_sc[hh, slot] - m_new).astype(BF16)
            v_blk = jnp.concatenate([vt_ref[pl.ds(hh * HEAD_DIM, HEAD_DIM), pl.ds(ks, blk)], ones_blk], axis=0)
            acc_sc[hh] = alpha * acc_sc[hh] + _dot(v_blk, p)
            m_sc[hh] = m_new

    @pl.when(qi == 0)
    def _():
        scores(0, 0, True)
        softmax_pv(0, 0)

    @pl.when(qi > 0)
    def _():
        scores(0, 0, False)
        n_pairs = (qi - 1) // 2

        def body(k, carry):
            scores(2 * k + 1, 1, False)
            softmax_pv(2 * k, 0)
            scores(2 * k + 2, 0, False)
            softmax_pv(2 * k + 1, 1)
            return carry

        lax.fori_loop(0, n_pairs, body, 0)
        i0 = 2 * n_pairs

        @pl.when(qi - i0 == 2)
        def _():
            scores(i0 + 1, 1, False)
            softmax_pv(i0, 0)
            scores(qi, 0, True)
            softmax_pv(i0 + 1, 1)
            softmax_pv(qi, 0)

        @pl.when(qi - i0 == 1)
        def _():
            scores(qi, 1, True)
            softmax_pv(i0, 0)
            softmax_pv(qi, 1)

    for hh in range(ATTN_HEADS):
        acc = acc_sc[hh]
        o_ref[pl.ds(hh * HEAD_DIM, HEAD_DIM), :] = (acc[:HEAD_DIM] / acc[HEAD_DIM:HEAD_DIM + 1]).astype(o_ref.dtype)


def _prompt_attn(qt, ka, vt, blk):
    d, s = qt.shape
    rows = ATTN_HEADS * HEAD_DIM
    return pl.pallas_call(
        functools.partial(_prompt_attn_kernel, blk=blk),
        grid=(N_HEADS // ATTN_HEADS, s // blk),
        in_specs=[pl.BlockSpec((rows, blk), lambda h, i: (h, i)),
                  pl.BlockSpec((ATTN_HEADS, s, 2 * HEAD_DIM), lambda h, i: (h, 0, 0)),
                  pl.BlockSpec((rows, s), lambda h, i: (h, 0))],
        out_specs=pl.BlockSpec((rows, blk), lambda h, i: (h, i)),
        out_shape=jax.ShapeDtypeStruct((d, s), BF16),
        scratch_shapes=[pltpu.VMEM((ATTN_HEADS, 2, blk, blk), F32), pltpu.VMEM((ATTN_HEADS, 2, 1, blk), F32),
                        pltpu.VMEM((ATTN_HEADS, 1, blk), F32),
                        pltpu.VMEM((ATTN_HEADS, HEAD_DIM + BF16_SUBLANES, blk), F32)],
        compiler_params=_params("parallel", "parallel"),
        name="fox_prompt_attn",
    )(qt, ka, vt)


def _sample_attn_kernel(pt_ref, *refs, n_pp, page, t_new):
    del pt_ref
    k_refs = refs[:n_pp]
    v_refs = refs[n_pp:2 * n_pp]
    lf_refs = refs[2 * n_pp:3 * n_pp]
    q_ref, kn_ref, vn_ref, lfn_ref, o_ref, qbd_sc, m_sc, l_sc, acc_sc, suf_sc = refs[3 * n_pp:]
    j = pl.program_id(1)
    rows = N_HEADS * t_new
    d = N_HEADS * HEAD_DIM
    own = (lax.broadcasted_iota(jnp.int32, (rows, d), 1) // HEAD_DIM
           == lax.broadcasted_iota(jnp.int32, (rows, d), 0) // t_new)

    def online_update(s, pv_fn):
        m_prev = m_sc[...]
        m_new = jnp.maximum(m_prev, jnp.max(s, axis=1, keepdims=True))
        alpha = jnp.exp(m_prev - m_new)
        p = jnp.exp(s - m_new)
        l_sc[...] = alpha * l_sc[...] + jnp.sum(p, axis=1, keepdims=True)
        acc_sc[...] = alpha * acc_sc[...] + pv_fn(p.astype(BF16))
        m_sc[...] = m_new

    def add_head_rows(s, bias):
        return jnp.concatenate(
            [s[h * t_new:(h + 1) * t_new, :] + bias[h:h + 1, :] for h in range(N_HEADS)], axis=0)

    @pl.when(j == 0)
    def _():
        m_sc[...] = jnp.full_like(m_sc, NEG_BIG)
        l_sc[...] = jnp.zeros_like(l_sc)
        acc_sc[...] = jnp.zeros_like(acc_sc)
        suf_sc[...] = jnp.zeros_like(suf_sc)
        q_rep = jnp.concatenate([q_ref[...]] * N_HEADS, axis=0)
        qbd_sc[...] = jnp.where(own, q_rep, 0.0).astype(BF16)
        r8 = lax.broadcasted_iota(jnp.int32, (t_new, t_new), 0)
        c8 = lax.broadcasted_iota(jnp.int32, (t_new, t_new), 1)
        cum = _dot_exact_rhs((c8 <= r8).astype(BF16), lfn_ref[...])
        cum_pad = jnp.concatenate([cum, jnp.zeros((LANES - t_new, LANES), F32)], axis=0)
        eye = (lax.broadcasted_iota(jnp.int32, (N_HEADS, LANES), 0)
               == lax.broadcasted_iota(jnp.int32, (N_HEADS, LANES), 1)).astype(BF16)
        hi, mid, lo = _split3(cum_pad)
        cum_t = (_dot_nt(eye, hi) + _dot_nt(eye, mid)) + _dot_nt(eye, lo)
        zpad = jnp.zeros((LANES - t_new, d), F32)
        kn = jnp.concatenate([kn_ref[...], zpad], axis=0).astype(BF16)
        vn = jnp.concatenate([vn_ref[...], zpad], axis=0).astype(BF16)
        s = add_head_rows(_dot_nt(qbd_sc[...], kn), -cum_t)
        qt = lax.broadcasted_iota(jnp.int32, (rows, LANES), 0) % t_new
        kt = lax.broadcasted_iota(jnp.int32, (rows, LANES), 1)
        s = jnp.where(kt <= qt, s, NEG_BIG)
        online_update(s, lambda p: _dot(p, vn))

    rp = lax.broadcasted_iota(jnp.int32, (page, page), 0)
    cp = lax.broadcasted_iota(jnp.int32, (page, page), 1)
    newer = (rp > cp).astype(BF16)
    carry = suf_sc[...]
    suf_pages = [None] * n_pp
    for i in reversed(range(n_pp)):
        lf = lf_refs[i][...]
        suf_pages[i] = _dot_exact_lhs(lf, newer) + carry
        carry = carry + jnp.sum(lf, axis=1, keepdims=True)
    suf_sc[...] = carry
    suf = jnp.concatenate(suf_pages, axis=1)

    kt_pages = jnp.concatenate([r[...] for r in k_refs], axis=1).astype(BF16)
    vt_pages = jnp.concatenate([r[...] for r in v_refs], axis=1).astype(BF16)
    s = add_head_rows(_dot(qbd_sc[...], kt_pages), suf)
    online_update(s, lambda p: _dot_nt(p, vt_pages))

    @pl.when(j == pl.num_programs(1) - 1)
    def _():
        out = jnp.where(own, acc_sc[...] / l_sc[...], 0.0)
        o = out[0:t_new, :]
        for h in range(1, N_HEADS):
            o = o + out[h * t_new:(h + 1) * t_new, :]
        o_ref[...] = o


def _sample_attn(page_table, cache_kt, cache_vt, cache_lft, q_s, k_s, v_s, lf_s, n_pp):
    n_seq, n_pages = page_table.shape
    _, d, page = cache_kt.shape
    t_new = q_s.shape[0] // n_seq
    n_grp = n_pages // n_pp

    def page_idx(i):
        return lambda b, j, pt: (pt[b, (n_grp - 1 - j) * n_pp + i], 0, 0)

    kv_specs = [pl.BlockSpec((None, d, page), page_idx(i)) for i in range(n_pp)]
    lf_specs = [pl.BlockSpec((None, N_HEADS, page), page_idx(i)) for i in range(n_pp)]
    tok_spec = pl.BlockSpec((t_new, d), lambda b, j, pt: (b, 0))
    rows = N_HEADS * t_new
    grid_spec = pltpu.PrefetchScalarGridSpec(
        num_scalar_prefetch=1,
        grid=(n_seq, n_grp),
        in_specs=kv_specs + kv_specs + lf_specs + [
            tok_spec, tok_spec, tok_spec, pl.BlockSpec((t_new, LANES), lambda b, j, pt: (b, 0))],
        out_specs=tok_spec,
        scratch_shapes=[pltpu.VMEM((rows, d), BF16), pltpu.VMEM((rows, 1), F32), pltpu.VMEM((rows, 1), F32),
                        pltpu.VMEM((rows, d), F32), pltpu.VMEM((N_HEADS, 1), F32)],
    )
    return pl.pallas_call(
        functools.partial(_sample_attn_kernel, n_pp=n_pp, page=page, t_new=t_new),
        grid_spec=grid_spec,
        out_shape=jax.ShapeDtypeStruct((n_seq * t_new, d), F32),
        compiler_params=_params("parallel", "arbitrary"),
        name="fox_sample_attn",
    )(page_table, *([cache_kt] * n_pp), *([cache_vt] * n_pp), *([cache_lft] * n_pp), q_s, k_s, v_s, lf_s)


def _oproj_kernel(a_ref, x_ref, wo_ref, o_ref, *, feature_major):
    a = a_ref[...].astype(BF16)
    if feature_major:
        proj = lax.dot_general(a, wo_ref[...], _TN, preferred_element_type=F32)
    else:
        proj = _dot(a, wo_ref[...])
    o_ref[...] = x_ref[...] + proj


def _oproj(attn, x, wo, tm, feature_major):
    n, d = x.shape
    row_spec = pl.BlockSpec((tm, d), lambda i: (i, 0))
    attn_spec = pl.BlockSpec((d, tm), lambda i: (0, i)) if feature_major else row_spec
    return pl.pallas_call(
        functools.partial(_oproj_kernel, feature_major=feature_major),
        grid=(n // tm,),
        in_specs=[attn_spec, row_spec, _full_spec((d, d))],
        out_specs=row_spec,
        out_shape=jax.ShapeDtypeStruct((n, d), F32),
        compiler_params=_params("parallel"),
        name="attn_out_proj",
    )(attn, x, wo)


def _swiglu_act(x_bf16, wg_ref, wu_ref):
    gate = _dot(x_bf16, wg_ref[...])
    return gate * jax.nn.sigmoid(gate) * _dot(x_bf16, wu_ref[...])


def _ffn_kernel(x_ref, g_ref, wg_ref, wu_ref, wd_ref, o_ref, xn_sc):
    j = pl.program_id(1)

    @pl.when(j == 0)
    def _():
        x = x_ref[...]
        xn_sc[...] = _rmsnorm(x, g_ref[...]).astype(BF16)
        o_ref[...] = x

    act = _swiglu_act(xn_sc[...], wg_ref, wu_ref).astype(BF16)
    o_ref[...] += _dot(act, wd_ref[...])


def _ffn(x, gain, w_gate_up, w_down, tm, tf):
    n, d = x.shape
    n_f = w_down.shape[0] // tf
    row_spec = pl.BlockSpec((tm, d), lambda i, j: (i, 0))
    return pl.pallas_call(
        _ffn_kernel,
        grid=(n // tm, n_f),
        in_specs=[row_spec, pl.BlockSpec((1, d), lambda i, j: (0, 0)),
                  pl.BlockSpec((d, tf), lambda i, j: (0, j)),
                  pl.BlockSpec((d, tf), lambda i, j: (0, j + n_f)),
                  pl.BlockSpec((tf, d), lambda i, j: (j, 0))],
        out_specs=row_spec,
        out_shape=jax.ShapeDtypeStruct((n, d), F32),
        scratch_shapes=[pltpu.VMEM((tm, d), BF16)],
        compiler_params=_params("parallel", "arbitrary"),
        name="dense_swiglu",
    )(x, gain, w_gate_up, w_gate_up, w_down)


def _pool_mix(ext_ref, x, u, pos0, wgrp_ref, scale_ref, o_ref, n_rows):
    gdim = u.shape[1] // len(POOL_WINDOWS)
    pos = pos0 + lax.broadcasted_iota(jnp.int32, (n_rows, 1), 0)
    for g, w in enumerate(POOL_WINDOWS):
        cols = pl.ds(g * gdim, gdim)
        win = ext_ref[pl.ds(POOL_PAD, n_rows), cols]
        for i in range(1, w):
            win = win + ext_ref[pl.ds(POOL_PAD - i, n_rows), cols]
        count = jnp.minimum(pos + 1, w).astype(F32)
        pooled = (win / count - u[:, g * gdim:(g + 1) * gdim]).astype(BF16)
        mixed = _dot(pooled, wgrp_ref[g]) * scale_ref[:, cols]
        o_ref[:, cols] = x[:, g * gdim:(g + 1) * gdim] + mixed


def _pool_prompt_kernel(x_ref, g_ref, wgrp_ref, scale_ref, o_ref, tail_ref, ext_sc, *, tm):
    i = pl.program_id(0)

    @pl.when(i == 0)
    def _():
        ext_sc[pl.ds(0, POOL_PAD), :] = jnp.zeros((POOL_PAD, ext_sc.shape[1]), F32)

    @pl.when(i > 0)
    def _():
        ext_sc[pl.ds(0, POOL_PAD), :] = ext_sc[pl.ds(tm, POOL_PAD), :]

    x = x_ref[...]
    u = _rmsnorm(x, g_ref[...])
    ext_sc[pl.ds(POOL_PAD, tm), :] = u
    tail_ref[...] = u[tm - POOL_PAD:, :]
    _pool_mix(ext_sc, x, u, i * tm, wgrp_ref, scale_ref, o_ref, tm)


def _pool_prompt(x, gain, w_group, scale, tm):
    n, d = x.shape
    gdim = d // len(POOL_WINDOWS)
    row_spec = pl.BlockSpec((tm, d), lambda i: (i, 0))
    return pl.pallas_call(
        functools.partial(_pool_prompt_kernel, tm=tm),
        grid=(n // tm,),
        in_specs=[row_spec, pl.BlockSpec((1, d), lambda i: (0, 0)),
                  pl.BlockSpec((len(POOL_WINDOWS), gdim, gdim), lambda i: (0, 0, 0)),
                  pl.BlockSpec((1, d), lambda i: (0, 0))],
        out_specs=[row_spec, pl.BlockSpec((POOL_PAD, d), lambda i: (0, 0))],
        out_shape=[jax.ShapeDtypeStruct((n, d), F32), jax.ShapeDtypeStruct((POOL_PAD, d), F32)],
        scratch_shapes=[pltpu.VMEM((POOL_PAD + tm, d), F32)],
        compiler_params=_params("arbitrary"),
        name="pool_mix_prompt",
    )(x, gain, w_group, scale)


def _pool_sample_kernel(x_ref, st_ref, g_ref, wgrp_ref, scale_ref, o_ref, hist_ref, ext_sc, *, t_new, past_len):
    x = x_ref[...]
    u = _rmsnorm(x, g_ref[...])
    ext_sc[pl.ds(0, POOL_PAD), :] = st_ref[...]
    ext_sc[pl.ds(POOL_PAD, t_new), :] = u
    hist_ref[...] = ext_sc[pl.ds(t_new, POOL_PAD), :]
    _pool_mix(ext_sc, x, u, past_len, wgrp_ref, scale_ref, o_ref, t_new)


def _pool_sample(x, state_pad, gain, w_group, scale, t_new, past_len):
    n, d = x.shape
    gdim = d // len(POOL_WINDOWS)
    row_spec = pl.BlockSpec((t_new, d), lambda b: (b, 0))
    st_spec = pl.BlockSpec((None, POOL_PAD, d), lambda b: (b, 0, 0))
    return pl.pallas_call(
        functools.partial(_pool_sample_kernel, t_new=t_new, past_len=past_len),
        grid=(n // t_new,),
        in_specs=[row_spec, st_spec, pl.BlockSpec((1, d), lambda b: (0, 0)),
                  pl.BlockSpec((len(POOL_WINDOWS), gdim, gdim), lambda b: (0, 0, 0)),
                  pl.BlockSpec((1, d), lambda b: (0, 0))],
        out_specs=[row_spec, st_spec],
        out_shape=[jax.ShapeDtypeStruct((n, d), F32), jax.ShapeDtypeStruct(state_pad.shape, F32)],
        scratch_shapes=[pltpu.VMEM((POOL_PAD + t_new, d), F32)],
        compiler_params=_params("parallel"),
        name="pool_mix_sample",
    )(x, state_pad, gain, w_group, scale)


def _moe_kernel(x_ref, g_ref, wrt_ref, wg_ref, wu_ref, wd_ref, gf_ref, o_ref,
                xn_sc, gate_sc, rank_sc, xe_sc, ge_sc, ye_sc, *, c0):
    e = pl.program_id(1)
    j = pl.program_id(2)
    tm = x_ref.shape[0]
    e_rows = gate_sc.shape[0]

    @pl.when((e == 0) & (j == 0))
    def _():
        xn = _rmsnorm(x_ref[...], g_ref[...])
        xh, xm, xl = _split3(xn)
        xn_sc[...] = xh
        wh, wm, wl = wrt_ref[0], wrt_ref[1], wrt_ref[2]
        logits = ((_dot_nt(wl, xh) + _dot_nt(wm, xm) + _dot_nt(wh, xl))
                  + (_dot_nt(wm, xh) + _dot_nt(wh, xm))) + _dot_nt(wh, xh)
        erow = lax.broadcasted_iota(jnp.int32, (e_rows, tm), 0)
        logits = jnp.where(erow < N_EXPERTS, logits, -jnp.inf)
        top1 = jnp.max(logits, axis=0, keepdims=True)
        idx1 = jnp.min(jnp.where(logits == top1, erow, e_rows), axis=0, keepdims=True)
        rest = jnp.where(erow == idx1, -jnp.inf, logits)
        top2 = jnp.max(rest, axis=0, keepdims=True)
        idx2 = jnp.min(jnp.where(rest == top2, erow, e_rows), axis=0, keepdims=True)
        e2 = jnp.exp(top2 - top1)
        denom = 1.0 + e2
        gate_sc[...] = jnp.where(erow == idx1, 1.0 / denom, 0.0) + jnp.where(erow == idx2, e2 / denom, 0.0)
        routed = (erow == idx1) | (erow == idx2)
        earlier = jnp.where(lax.broadcasted_iota(jnp.int32, (tm, tm), 0)
                            < lax.broadcasted_iota(jnp.int32, (tm, tm), 1), 1.0, 0.0).astype(BF16)
        rank = _dot(jnp.where(routed, 1.0, 0.0).astype(BF16), earlier)
        rank_sc[...] = jnp.where(routed, rank, -1.0)
        o_ref[...] = x_ref[...]

    rank_e = rank_sc[pl.ds(e, 1), :]
    count = jnp.sum(jnp.where(rank_e >= 0.0, 1.0, 0.0)).astype(jnp.int32)
    n_extra = lax.shift_right_logical(jnp.maximum(count - c0, 0) + (MOE_CHUNK - 1), MOE_CHUNK.bit_length() - 1)

    def for_chunks(fn):
        fn(0, c0)

        def body(k, carry):
            fn(pl.multiple_of(c0 + k * MOE_CHUNK, BF16_SUBLANES), MOE_CHUNK)
            return carry

        lax.fori_loop(0, n_extra, body, 0)

    def selected(base, rows, value):
        slot = (base + lax.broadcasted_iota(jnp.int32, (rows, 1), 0)).astype(F32)
        return jnp.where(rank_e == slot, value, 0.0)

    def selection(base, rows):
        return selected(base, rows, 1.0).astype(BF16)

    def gather(base, rows):
        xe_sc[pl.ds(base, rows), :] = _dot(selection(base, rows), xn_sc[...]).astype(BF16)
        gate_e = gate_sc[pl.ds(e, 1), :]
        ge_sc[pl.ds(base, rows), :] = jnp.sum(selected(base, rows, gate_e), axis=1, keepdims=True)
        ye_sc[pl.ds(base, rows), :] = jnp.zeros((rows, ye_sc.shape[1]), F32)

    def expert(base, rows):
        act = _swiglu_act(xe_sc[pl.ds(base, rows), :], wg_ref, wu_ref) * ge_sc[pl.ds(base, rows), :]
        ye_sc[pl.ds(base, rows), :] += _dot(act.astype(BF16), wd_ref[...])

    def scatter(base, rows):
        y = ye_sc[pl.ds(base, rows), :].astype(BF16)
        o_ref[...] += lax.dot_general(selection(base, rows), y, _TN, preferred_element_type=F32)

    @pl.when(j == 0)
    def _():
        for_chunks(gather)

    for_chunks(expert)

    @pl.when(j == pl.num_programs(2) - 1)
    def _():
        for_chunks(scatter)

    @pl.when((e == pl.num_programs(1) - 1) & (j == pl.num_programs(2) - 1))
    def _():
        o_ref[...] = _rmsnorm(o_ref[...], gf_ref[...])


def _moe(x, gain, w_router_t3, w_gate_up, w_down, gain_final, tm, tf):
    n, d = x.shape
    n_e, d_ff, _ = w_down.shape
    n_f = d_ff // tf
    assert tm % MOE_CHUNK == 0
    mean_rows = tm * 2 // n_e
    c0 = max(MOE_CHUNK, -(-(mean_rows + mean_rows // 8) // BF16_SUBLANES) * BF16_SUBLANES)
    cap = c0 + -(-(tm - c0) // MOE_CHUNK) * MOE_CHUNK
    row_spec = pl.BlockSpec((tm, d), lambda i, e, j: (i, 0), pipeline_mode=pl.Buffered(1))
    vec_spec = pl.BlockSpec((1, d), lambda i, e, j: (0, 0))
    return pl.pallas_call(
        functools.partial(_moe_kernel, c0=c0),
        grid=(n // tm, n_e, n_f),
        in_specs=[row_spec, vec_spec,
                  pl.BlockSpec((N_SPLIT, LANES, d), lambda i, e, j: (0, 0, 0)),
                  pl.BlockSpec((None, d, tf), lambda i, e, j: (e, 0, j)),
                  pl.BlockSpec((None, d, tf), lambda i, e, j: (e, 0, j + n_f)),
                  pl.BlockSpec((None, tf, d), lambda i, e, j: (e, j, 0)),
                  vec_spec],
        out_specs=row_spec,
        out_shape=jax.ShapeDtypeStruct((n, d), F32),
        scratch_shapes=[pltpu.VMEM((tm, d), BF16), pltpu.VMEM((LANES, tm), F32), pltpu.VMEM((LANES, tm), F32),
                        pltpu.VMEM((cap, d), BF16), pltpu.VMEM((cap, 1), F32), pltpu.VMEM((cap, d), F32)],
        compiler_params=_params("parallel", "arbitrary", "arbitrary"),
        name="moe_swiglu_final_norm",
    )(x, gain, w_router_t3, w_gate_up, w_gate_up, w_down, gain_final)


def _row_tile(n, target):
    t = min(n, target)
    while n % t:
        t //= 2
    return t


def kernel(x_prompt, x_sample, cache_k, cache_v, cache_logf, state_pool, page_table, l0_norm_attn, l0_w_qkvf, l0_b_f, l0_w_o, l0_norm_ffn, l0_w_gate_up, l0_w_down, l1_norm_pool, l1_w_group, l1_pool_scale, l1_norm_ffn, l1_w_router, l1_w_gate_up, l1_w_down, final_norm):
    b_p, s_p, d = x_prompt.shape
    b_s, t_new, _ = x_sample.shape
    assert b_p == 1 and d == N_HEADS * HEAD_DIM
    n_phys, page = cache_k.shape[:2]
    past_len = page_table.shape[1] * page
    d_ff = l0_w_down.shape[0]

    row = lambda v: v.reshape(1, -1).astype(F32)
    w_t = l0_w_qkvf.T.astype(BF16)
    wf_pad = jnp.pad(w_t[3 * d:], ((0, LANES - N_HEADS), (0, 0)))
    bf = jnp.pad(l0_b_f, (0, LANES - N_HEADS)).reshape(1, LANES)
    bft = l0_b_f.reshape(N_HEADS, 1)
    wo = l0_w_o.astype(BF16)
    tf = d_ff // 2 if d_ff % (2 * MXU_WIDTH) == 0 else d_ff
    w0_gu = l0_w_gate_up.astype(BF16)
    w0_d = l0_w_down.astype(BF16)
    wgrp = l1_w_group.astype(BF16)
    wr_pad = jnp.pad(l1_w_router.T, ((0, LANES - N_EXPERTS), (0, 0)))
    wr_hi = wr_pad.astype(BF16)
    wr_r1 = wr_pad - wr_hi.astype(F32)
    wr_mid = wr_r1.astype(BF16)
    wr_lo = (wr_r1 - wr_mid.astype(F32)).astype(BF16)
    wr3 = jnp.stack([wr_hi, wr_mid, wr_lo])
    w1_gu = l1_w_gate_up.astype(BF16)
    w1_d = l1_w_down.astype(BF16)

    xp = x_prompt.reshape(s_p, d)
    xs = x_sample.reshape(b_s * t_new, d)
    n_s = b_s * t_new

    blk = _row_tile(s_p, 512)
    qt_p, kh_p, kt_p, vt_p, vtb_p, lf_p, lft_p = _qkvf_prompt(xp, row(l0_norm_attn), w_t, wf_pad, bf, bft, blk)
    q_s, k_s, v_s, lf_s = _qkvf_sample(xs, row(l0_norm_attn), w_t, wf_pad, bf, _row_tile(n_s, 512))

    attn_p = _prompt_attn(qt_p, _key_aug(lf_p, kh_p, blk), vtb_p, _row_tile(s_p, ATTN_BLOCK))

    feat_major = lambda c: jnp.transpose(c, (0, 2, 3, 1)).reshape(n_phys, d, page)
    n_pp = math.gcd(page_table.shape[1], PAGES_PER_STEP)
    attn_s = _sample_attn(page_table, feat_major(cache_k), feat_major(cache_v), jnp.swapaxes(cache_logf, 1, 2),
                          q_s, k_s, v_s, lf_s, n_pp)

    hp = _oproj(attn_p, xp, wo, blk, True)
    hs = _oproj(attn_s, xs, wo, _row_tile(n_s, 512), False)

    hp = _ffn(hp, row(l0_norm_ffn), w0_gu, w0_d, _row_tile(s_p, 1024), tf)
    hs = _ffn(hs, row(l0_norm_ffn), w0_gu, w0_d, _row_tile(n_s, 1024), tf)

    hp, tail_p = _pool_prompt(hp, row(l1_norm_pool), wgrp, row(l1_pool_scale), _row_tile(s_p, 512))
    state_pad = jnp.pad(state_pool, ((0, 0), (POOL_PAD - POOL_HIST, 0), (0, 0)))
    hs, hist_s = _pool_sample(hs, state_pad, row(l1_norm_pool), wgrp, row(l1_pool_scale), t_new, past_len)

    yp = _moe(hp, row(l1_norm_ffn), wr3, w1_gu, w1_d, row(final_norm), _row_tile(s_p, 1024), tf)
    ys = _moe(hs, row(l1_norm_ffn), wr3, w1_gu, w1_d, row(final_norm), _row_tile(n_s, 1024), tf)

    heads_t = lambda a: jnp.transpose(a.reshape(1, N_HEADS, HEAD_DIM, s_p), (0, 3, 1, 2))
    heads = lambda a: a.reshape(b_s, t_new, N_HEADS, HEAD_DIM)
    return (yp.reshape(b_p, s_p, d), ys.reshape(b_s, t_new, d),
            heads_t(kt_p), heads_t(vt_p), jnp.transpose(lft_p.reshape(1, N_HEADS, s_p), (0, 2, 1)),
            tail_p[POOL_PAD - POOL_HIST:].reshape(b_p, POOL_HIST, d),
            heads(k_s), heads(v_s), lf_s[:, :N_HEADS].reshape(b_s, t_new, N_HEADS),
            hist_s[:, POOL_PAD - POOL_HIST:, :])
```
